```python
import jax, jax.numpy as jnp
from jax import lax
import numpy as np

D_MODEL = 2048
BATCH = 4
SEQ = 2048
DEPTH = 2

CONV_DIM = D_MODEL // 2
CONV_GROUPS = 8
CONV_GROUP_DIM = CONV_DIM // CONV_GROUPS
CONV_WIDTH = 31
CONV_PAD = (CONV_WIDTH - 1) // 2
SGU_DIM = D_MODEL // 2
SGU_HEADS = 8
SGU_HEAD_DIM = SGU_DIM // SGU_HEADS
CHUNK = 128
IN_DIM = 2 * CONV_DIM + 2 * SGU_DIM
MIX_DIM = CONV_DIM + SGU_DIM
FNET_GROUPS = 8
FNET_GROUP_DIM = D_MODEL // FNET_GROUPS
D_FF = ((8 * D_MODEL + 3 * 256 - 1) // (3 * 256)) * 256
N_EVEN = (DEPTH + 1) // 2
N_ODD = DEPTH // 2
RMS_EPS = 1e-6
LN_EPS = 1e-5

kernel_name = "hybrid_conv_sgu_fnet_encoder"


def rms_norm(x, g):
    xf = x.astype(jnp.float32)
    y = xf * lax.rsqrt(jnp.mean(xf * xf, axis=-1, keepdims=True) + RMS_EPS)
    return (y * g.astype(jnp.float32)).astype(x.dtype)


def layer_norm(x, g, b):
    xf = x.astype(jnp.float32)
    mu = jnp.mean(xf, axis=-1, keepdims=True)
    xc = xf - mu
    y = xc * lax.rsqrt(jnp.mean(xc * xc, axis=-1, keepdims=True) + LN_EPS)
    return (y * g.astype(jnp.float32) + b.astype(jnp.float32)).astype(x.dtype)


def conv_module(z, dw_w, dw_b, ln_g, ln_b):
    a, gate = jnp.split(z, 2, axis=-1)
    y = a * jax.nn.sigmoid(gate)
    y = lax.conv_general_dilated(
        y, dw_w[:, None, :], window_strides=(1,),
        padding=[(CONV_PAD, CONV_PAD)],
        dimension_numbers=("NWC", "WIO", "NWC"),
        feature_group_count=CONV_DIM) + dw_b
    b, s, _ = y.shape
    y = layer_norm(y.reshape(b, s, CONV_GROUPS, CONV_GROUP_DIM),
                   ln_g.reshape(CONV_GROUPS, CONV_GROUP_DIM),
                   ln_b.reshape(CONV_GROUPS, CONV_GROUP_DIM)).reshape(b, s, CONV_DIM)
    return jax.nn.silu(y)


def spatial_gating(z, ln_g, ln_b, w_s, b_s):
    z = jax.nn.gelu(z, approximate=False)
    u, v = jnp.split(z, 2, axis=-1)
    v = layer_norm(v, ln_g, ln_b)
    b, s, _ = v.shape
    v = v.reshape(b, s // CHUNK, CHUNK, SGU_HEADS, SGU_HEAD_DIM)
    mixed = jnp.einsum("hpq,bnqhc->bnphc", w_s, v) + b_s.T[None, None, :, :, None]
    return u * mixed.reshape(b, s, SGU_DIM)


def fourier_mix(h):
    b, s, d = h.shape
    hg = h.astype(jnp.float32).reshape(b, s, FNET_GROUPS, FNET_GROUP_DIM)
    y = jnp.fft.fft2(hg, axes=(1, 3), norm="ortho").real
    return y.reshape(b, s, d).astype(h.dtype)


def swiglu(h, w_gate, w_up, w_down):
    return (jax.nn.silu(h @ w_gate) * (h @ w_up)) @ w_down


def setup_inputs(seed: int = 0) -> dict:
    key = jax.random.key(seed)
    ks = jax.random.split(key, 24)
    f32 = jnp.float32

    def nrm(k, shape, scale):
        return jax.random.normal(k, shape, f32) * scale

    def gain(k, shape):
        return 1.0 + 0.02 * jax.random.normal(k, shape, f32)

    return {
        "x": jax.random.normal(ks[0], (BATCH, SEQ, D_MODEL), f32),
        "mix_norm_g": gain(ks[1], (DEPTH, D_MODEL)),
        "ffn_norm_g": gain(ks[2], (DEPTH, D_MODEL)),
        "final_norm_g": gain(ks[3], (D_MODEL,)),
        "ab_w_in": nrm(ks[4], (N_EVEN, D_MODEL, IN_DIM), D_MODEL ** -0.5),
        "conv_dw_w": nrm(ks[5], (N_EVEN, CONV_WIDTH, CONV_DIM), CONV_WIDTH ** -0.5),
        "conv_dw_b": nrm(ks[6], (N_EVEN, CONV_DIM), 0.02),
        "conv_ln_g": gain(ks[7], (N_EVEN, CONV_DIM)),
        "conv_ln_b": nrm(ks[8], (N_EVEN, CONV_DIM), 0.02),
        "sgu_ln_g": gain(ks[9], (N_EVEN, SGU_DIM)),
        "sgu_ln_b": nrm(ks[10], (N_EVEN, SGU_DIM), 0.02),
        "sgu_w": nrm(ks[11], (N_EVEN, SGU_HEADS, CHUNK, CHUNK), CHUNK ** -0.5),
        "sgu_b": 1.0 + nrm(ks[12], (N_EVEN, SGU_HEADS, CHUNK), 0.01),
        "ab_w_out": nrm(ks[13], (N_EVEN, MIX_DIM, D_MODEL), MIX_DIM ** -0.5),
        "fnet_w_out": nrm(ks[14], (N_ODD, D_MODEL, D_MODEL), D_MODEL ** -0.5),
        "fnet_b_out": nrm(ks[15], (N_ODD, D_MODEL), 0.02),
        "ffn_w_gate": nrm(ks[16], (DEPTH, D_MODEL, D_FF), D_MODEL ** -0.5),
        "ffn_w_up": nrm(ks[17], (DEPTH, D_MODEL, D_FF), D_MODEL ** -0.5),
        "ffn_w_down": nrm(ks[18], (DEPTH, D_FF, D_MODEL), D_FF ** -0.5),
    }


def reference(x, mix_norm_g, ffn_norm_g, final_norm_g, ab_w_in, conv_dw_w,
              conv_dw_b, conv_ln_g, conv_ln_b, sgu_ln_g, sgu_ln_b, sgu_w, sgu_b,
              ab_w_out, fnet_w_out, fnet_b_out, ffn_w_gate, ffn_w_up, ffn_w_down):
    for layer in range(DEPTH):
        h = rms_norm(x, mix_norm_g[layer])
        if layer % 2 == 0:
            i = layer // 2
            z = h @ ab_w_in[i]
            z_conv = z[..., : 2 * CONV_DIM]
            z_sgu = z[..., 2 * CONV_DIM:]
            y_conv = conv_module(z_conv, conv_dw_w[i], conv_dw_b[i],
                                 conv_ln_g[i], conv_ln_b[i])
            y_sgu = spatial_gating(z_sgu, sgu_ln_g[i], sgu_ln_b[i],
                                   sgu_w[i], sgu_b[i])
            x = x + jnp.concatenate([y_conv, y_sgu], axis=-1) @ ab_w_out[i]
        else:
            j = layer // 2
            x = x + fourier_mix(h) @ fnet_w_out[j] + fnet_b_out[j]
        h = rms_norm(x, ffn_norm_g[layer])
        x = x + swiglu(h, ffn_w_gate[layer], ffn_w_up[layer], ffn_w_down[layer])
    return rms_norm(x, final_norm_g)
```

```python
import functools
import math

import jax
import jax.numpy as jnp
from jax import lax
from jax.experimental import pallas as pl
from jax.experimental.pallas import tpu as pltpu

F32 = jnp.float32
BF16 = jnp.bfloat16

RMS_EPS = 1e-6
LN_EPS = 1e-5
CONV_GROUP_DIM = 128
CONV_WIDTH = 31
CONV_PAD = (CONV_WIDTH - 1) // 2
CONV_HALO = 16
SGU_HEADS = 8
CHUNK = 128
FNET_GROUPS = 8

V7X_VMEM_BYTES = 64 * 1024 * 1024
VMEM_LIMIT_BYTES = 56 * 1024 * 1024


def _params(*semantics):
    return pltpu.CompilerParams(dimension_semantics=semantics, vmem_limit_bytes=VMEM_LIMIT_BYTES)


def _rms(x, g):
    return x * lax.rsqrt(jnp.mean(x * x, axis=-1, keepdims=True) + RMS_EPS) * g


def _layer_norm(x, g, b):
    mu = jnp.mean(x, axis=-1, keepdims=True)
    xc = x - mu
    return xc * lax.rsqrt(jnp.mean(xc * xc, axis=-1, keepdims=True) + LN_EPS) * g + b


def _gelu(x):
    return 0.5 * x * (1.0 + lax.erf(x * (1.0 / math.sqrt(2.0))))


def _silu(x):
    return x * jax.nn.sigmoid(x)


def _norm_matmul_kernel(x_ref, g_ref, w_ref, o_ref, h_ref):
    @pl.when(pl.program_id(1) == 0)
    def _():
        h_ref[...] = _rms(x_ref[...], g_ref[...]).astype(BF16)

    o_ref[...] = jnp.dot(h_ref[...], w_ref[...], preferred_element_type=F32).astype(o_ref.dtype)


def _norm_matmul(x, g, w, *, tm, tn, out_dtype=F32):
    m, k = x.shape
    n = w.shape[1]
    return pl.pallas_call(
        _norm_matmul_kernel,
        grid=(m // tm, n // tn),
        in_specs=[
            pl.BlockSpec((tm, k), lambda i, j: (i, 0)),
            pl.BlockSpec((1, k), lambda i, j: (0, 0)),
            pl.BlockSpec((k, tn), lambda i, j: (0, j)),
        ],
        out_specs=pl.BlockSpec((tm, tn), lambda i, j: (i, j)),
        out_shape=jax.ShapeDtypeStruct((m, n), out_dtype),
        scratch_shapes=[pltpu.VMEM((tm, k), BF16)],
        compiler_params=_params("parallel", "arbitrary"),
        name="norm_matmul",
    )(x, g.reshape(1, k), w)


def _conv_kernel(a_ref, gate_ref, w_ref, b_ref, lg_ref, lb_ref, o_ref, pad_ref, *, seq, rows):
    zeros = jnp.zeros((CONV_HALO, CONV_GROUP_DIM), F32)
    pad_ref[0:CONV_HALO, :] = zeros
    pad_ref[CONV_HALO + seq:CONV_HALO + seq + CONV_HALO, :] = zeros
    pad_ref[CONV_HALO:CONV_HALO + seq, :] = a_ref[...] * jax.nn.sigmoid(gate_ref[...])
    w = w_ref[...]
    for c in range(seq // rows):
        base = c * rows + CONV_HALO - CONV_PAD
        acc = jnp.zeros((rows, CONV_GROUP_DIM), F32)
        for k in range(CONV_WIDTH):
            acc = acc + pad_ref[base + k:base + k + rows, :] * w[k:k + 1, :]
        y = _layer_norm(acc + b_ref[...], lg_ref[...], lb_ref[...])
        o_ref[c * rows:(c + 1) * rows, :] = _silu(y).astype(o_ref.dtype)


def _conv_module(z, dw_w, dw_b, ln_g, ln_b):
    bsz, seq, _ = z.shape
    cdim = dw_w.shape[1]
    ngroups = cdim // CONV_GROUP_DIM
    gd = CONV_GROUP_DIM
    vec = pl.BlockSpec((1, gd), lambda b, g: (0, g))
    return pl.pallas_call(
        functools.partial(_conv_kernel, seq=seq, rows=256),
        grid=(bsz, ngroups),
        in_specs=[
            pl.BlockSpec((None, seq, gd), lambda b, g: (b, 0, g)),
            pl.BlockSpec((None, seq, gd), lambda b, g: (b, 0, ngroups + g)),
            pl.BlockSpec((CONV_WIDTH, gd), lambda b, g: (0, g)),
            vec, vec, vec,
        ],
        out_specs=pl.BlockSpec((None, seq, gd), lambda b, g: (b, 0, g)),
        out_shape=jax.ShapeDtypeStruct((bsz, seq, cdim), BF16),
        scratch_shapes=[pltpu.VMEM((seq + 2 * CONV_HALO, gd), F32)],
        compiler_params=_params("parallel", "parallel"),
        name="conv_module",
    )(z, z, dw_w, dw_b.reshape(1, cdim), ln_g.reshape(1, cdim), ln_b.reshape(1, cdim))


def _sgu_kernel(zu_ref, zv_ref, lg_ref, lb_ref, ws_ref, bs_ref, o_ref, *, ts):
    hd = zu_ref.shape[-1] // SGU_HEADS
    for c in range(ts // CHUNK):
        rows = slice(c * CHUNK, (c + 1) * CHUNK)
        v = _layer_norm(_gelu(zv_ref[rows, :]), lg_ref[...], lb_ref[...]).astype(BF16)
        for h in range(SGU_HEADS):
            cols = slice(h * hd, (h + 1) * hd)
            mixed = jnp.dot(ws_ref[h], v[:, cols], preferred_element_type=F32) + bs_ref[:, h:h + 1]
            o_ref[rows, cols] = (_gelu(zu_ref[rows, cols]) * mixed).astype(o_ref.dtype)


def _spatial_gating(z, ln_g, ln_b, w_s, b_s, *, ts):
    bsz, seq, in_dim = z.shape
    gdim = ln_g.shape[0]
    ublk = (in_dim - 2 * gdim) // gdim
    return pl.pallas_call(
        functools.partial(_sgu_kernel, ts=ts),
        grid=(bsz, seq // ts),
        in_specs=[
            pl.BlockSpec((None, ts, gdim), lambda b, t: (b, t, ublk)),
            pl.BlockSpec((None, ts, gdim), lambda b, t: (b, t, ublk + 1)),
            pl.BlockSpec((1, gdim), lambda b, t: (0, 0)),
            pl.BlockSpec((1, gdim), lambda b, t: (0, 0)),
            pl.BlockSpec((SGU_HEADS, CHUNK, CHUNK), lambda b, t: (0, 0, 0)),
            pl.BlockSpec((CHUNK, SGU_HEADS), lambda b, t: (0, 0)),
        ],
        out_specs=pl.BlockSpec((None, ts, gdim), lambda b, t: (b, t, 0)),
        out_shape=jax.ShapeDtypeStruct((bsz, seq, gdim), BF16),
        compiler_params=_params("parallel", "parallel"),
        name="spatial_gating",
    )(z, z, ln_g.reshape(1, gdim), ln_b.reshape(1, gdim), w_s.astype(BF16), b_s.T)


def _proj_residual_kernel(*refs, n_lhs, has_bias):
    lhs_refs = refs[:n_lhs]
    w_ref, x_ref = refs[n_lhs], refs[n_lhs + 1]
    o_ref = refs[-1]
    acc = x_ref[...]
    if has_bias:
        acc = acc + refs[n_lhs + 2][...]
    k0 = 0
    for lhs_ref in lhs_refs:
        kw = lhs_ref.shape[-1]
        acc = acc + jnp.dot(lhs_ref[...], w_ref[k0:k0 + kw, :], preferred_element_type=F32)
        k0 += kw
    o_ref[...] = acc


def _proj_residual(lhs_list, w, x, bias=None, *, tm, tn):
    m, n = x.shape
    k = w.shape[0]
    in_specs = [pl.BlockSpec((tm, lhs.shape[1]), lambda i, j: (i, 0)) for lhs in lhs_list]
    in_specs += [pl.BlockSpec((k, tn), lambda i, j: (0, j)), pl.BlockSpec((tm, tn), lambda i, j: (i, j))]
    args = list(lhs_list) + [w, x]
    if bias is not None:
        in_specs.append(pl.BlockSpec((1, tn), lambda i, j: (0, j)))
        args.append(bias.reshape(1, n))
    return pl.pallas_call(
        functools.partial(_proj_residual_kernel, n_lhs=len(lhs_list), has_bias=bias is not None),
        grid=(m // tm, n // tn),
        in_specs=in_specs,
        out_specs=pl.BlockSpec((tm, tn), lambda i, j: (i, j)),
        out_shape=jax.ShapeDtypeStruct((m, n), F32),
        compiler_params=_params("parallel", "parallel"),
        name="proj_residual",
    )(*args)


def _ffn_kernel(x_ref, g_ref, wg_ref, wu_ref, wd_ref, fg_ref, o_ref, h_ref, *, final_norm):
    f = pl.program_id(1)

    @pl.when(f == 0)
    def _():
        x = x_ref[...]
        h_ref[...] = _rms(x, g_ref[...]).astype(BF16)
        o_ref[...] = x

    h = h_ref[...]
    gate = jnp.dot(h, wg_ref[...], preferred_element_type=F32)
    up = jnp.dot(h, wu_ref[...], preferred_element_type=F32)
    act = (_silu(gate) * up).astype(BF16)
    o_ref[...] += jnp.dot(act, wd_ref[...], preferred_element_type=F32)

    if final_norm:
        @pl.when(f == pl.num_programs(1) - 1)
        def _():
            o_ref[...] = _rms(o_ref[...], fg_ref[...])


def _ffn(x, g, w_gate, w_up, w_down, final_g, *, tm, tf, final_norm):
    m, d = x.shape
    dff = w_gate.shape[1]
    return pl.pallas_call(
        functools.partial(_ffn_kernel, final_norm=final_norm),
        grid=(m // tm, dff // tf),
        in_specs=[
            pl.BlockSpec((tm, d), lambda i, f: (i, 0)),
            pl.BlockSpec((1, d), lambda i, f: (0, 0)),
            pl.BlockSpec((d, tf), lambda i, f: (0, f)),
            pl.BlockSpec((d, tf), lambda i, f: (0, f)),
            pl.BlockSpec((tf, d), lambda i, f: (f, 0)),
            pl.BlockSpec((1, d), lambda i, f: (0, 0)),
        ],
        out_specs=pl.BlockSpec((tm, d), lambda i, f: (i, 0)),
        out_shape=jax.ShapeDtypeStruct((m, d), F32),
        scratch_shapes=[pltpu.VMEM((tm, d), BF16)],
        compiler_params=_params("parallel", "arbitrary"),
        name="ffn",
    )(x, g.reshape(1, d), w_gate, w_up, w_down, final_g.reshape(1, d))


def _fnet_channel_kernel(x_ref, g_ref, cs_ref, a_ref, b_ref):
    h = _rms(x_ref[...], g_ref[...]).astype(BF16)
    gd = cs_ref.shape[0]
    for grp in range(FNET_GROUPS):
        cols = slice(grp * gd, (grp + 1) * gd)
        ab = jnp.dot(h[:, cols], cs_ref[...], preferred_element_type=F32)
        a_ref[:, cols] = ab[:, :gd].astype(a_ref.dtype)
        b_ref[:, cols] = ab[:, gd:].astype(b_ref.dtype)


def _fnet_channel(x, g, cs, *, tm):
    m, d = x.shape
    gd = cs.shape[0]
    out = jax.ShapeDtypeStruct((m, d), BF16)
    return pl.pallas_call(
        _fnet_channel_kernel,
        grid=(m // tm,),
        in_specs=[
            pl.BlockSpec((tm, d), lambda i: (i, 0)),
            pl.BlockSpec((1, d), lambda i: (0, 0)),
            pl.BlockSpec((gd, 2 * gd), lambda i: (0, 0)),
        ],
        out_specs=[pl.BlockSpec((tm, d), lambda i: (i, 0))] * 2,
        out_shape=[out, out],
        compiler_params=_params("parallel"),
        name="fnet_channel",
    )(x, g.reshape(1, d), cs)


def _fnet_seq_kernel(c_ref, s_ref, a_ref, b_ref, o_ref):
    y = jnp.dot(c_ref[...], a_ref[...], preferred_element_type=F32)
    y = y - jnp.dot(s_ref[...], b_ref[...], preferred_element_type=F32)
    o_ref[...] = y.astype(o_ref.dtype)


def _fnet_seq(cmat, smat, a, b, *, tm, tn):
    bsz, seq, d = a.shape
    return pl.pallas_call(
        _fnet_seq_kernel,
        grid=(bsz, d // tn, seq // tm),
        in_specs=[
            pl.BlockSpec((tm, seq), lambda bb, j, i: (i, 0)),
            pl.BlockSpec((tm, seq), lambda bb, j, i: (i, 0)),
            pl.BlockSpec((None, seq, tn), lambda bb, j, i: (bb, 0, j)),
            pl.BlockSpec((None, seq, tn), lambda bb, j, i: (bb, 0, j)),
        ],
        out_specs=pl.BlockSpec((None, tm, tn), lambda bb, j, i: (bb, i, j)),
        out_shape=jax.ShapeDtypeStruct((bsz, seq, d), BF16),
        compiler_params=_params("parallel", "parallel", "parallel"),
        name="fnet_seq",
    )(cmat, smat, a, b)


def _dft_tables(n):
    idx = jnp.arange(n, dtype=jnp.int32)
    ang = ((idx[:, None] * idx[None, :]) % n).astype(F32) * (2.0 * math.pi / n)
    scale = 1.0 / math.sqrt(n)
    return jnp.cos(ang) * scale, jnp.sin(ang) * scale


def kernel(x, mix_norm_g, ffn_norm_g, final_norm_g, ab_w_in, conv_dw_w, conv_dw_b, conv_ln_g, conv_ln_b,
           sgu_ln_g, sgu_ln_b, sgu_w, sgu_b, ab_w_out, fnet_w_out, fnet_b_out, ffn_w_gate, ffn_w_up, ffn_w_down):
    bsz, seq, d = x.shape
    m = bsz * seq
    depth = mix_norm_g.shape[0]
    xf = x.reshape(m, d)
    for layer in range(depth):
        if layer % 2 == 0:
            i = layer // 2
            z = _norm_matmul(xf, mix_norm_g[layer], ab_w_in[i].astype(BF16), tm=1024, tn=1024)
            z = z.reshape(bsz, seq, -1)
            y_conv = _conv_module(z, conv_dw_w[i], conv_dw_b[i], conv_ln_g[i], conv_ln_b[i])
            y_sgu = _spatial_gating(z, sgu_ln_g[i], sgu_ln_b[i], sgu_w[i], sgu_b[i], ts=512)
            xf = _proj_residual([y_conv.reshape(m, -1), y_sgu.reshape(m, -1)], ab_w_out[i].astype(BF16), xf,
                                tm=1024, tn=1024)
        else:
            j = layer // 2
            gd = d // FNET_GROUPS
            cc, sc = _dft_tables(gd)
            cs, ss = _dft_tables(seq)
            a, b = _fnet_channel(xf, mix_norm_g[layer], jnp.concatenate([cc, sc], axis=1).astype(BF16), tm=512)
            y = _fnet_seq(cs.astype(BF16), ss.astype(BF16), a.reshape(bsz, seq, d), b.reshape(bsz, seq, d),
                          tm=1024, tn=1024)
            xf = _proj_residual([y.reshape(m, d)], fnet_w_out[j].astype(BF16), xf, fnet_b_out[j], tm=1024, tn=1024)
        xf = _ffn(xf, ffn_norm_g[layer], ffn_w_gate[layer].astype(BF16), ffn_w_up[layer].astype(BF16),
                  ffn_w_down[layer].astype(BF16), final_norm_g, tm=512, tf=512, final_norm=layer == depth - 1)
    return xf.reshape(bsz, seq, d)
```

```python
import functools
import math

import jax
import jax.numpy as jnp
from jax import lax
from jax.experimental import pallas as pl
from jax.experimental.pallas import tpu as pltpu

F32 = jnp.float32
BF16 = jnp.bfloat16

RMS_EPS = 1e-6
LN_EPS = 1e-5
CONV_GROUP_DIM = 128
CONV_WIDTH = 31
CONV_PAD = (CONV_WIDTH - 1) // 2
CONV_HALO = 16
SGU_HEADS = 8
CHUNK = 128
FNET_GROUPS = 8

V7X_VMEM_BYTES = 64 * 1024 * 1024
VMEM_LIMIT_BYTES = 56 * 1024 * 1024


def _params(*semantics):
    return pltpu.CompilerParams(dimension_semantics=semantics, vmem_limit_bytes=VMEM_LIMIT_BYTES)


def _rms(x, g):
    return x * lax.rsqrt(jnp.mean(x * x, axis=-1, keepdims=True) + RMS_EPS) * g


def _layer_norm(x, g, b):
    mu = jnp.mean(x, axis=-1, keepdims=True)
    xc = x - mu
    return xc * lax.rsqrt(jnp.mean(xc * xc, axis=-1, keepdims=True) + LN_EPS) * g + b


def _gelu(x):
    return 0.5 * x * (1.0 + lax.erf(x * (1.0 / math.sqrt(2.0))))


def _silu(x):
    return x * jax.nn.sigmoid(x)


def _norm_matmul_kernel(x_ref, g_ref, w_ref, o_ref, h_ref):
    @pl.when(pl.program_id(1) == 0)
    def _():
        h_ref[...] = _rms(x_ref[...], g_ref[...]).astype(BF16)

    o_ref[...] = jnp.dot(h_ref[...], w_ref[...].astype(BF16), preferred_element_type=F32).astype(o_ref.dtype)


def _norm_matmul(x, g, w, idx, *, tm, tn, out_dtype=F32):
    m, k = x.shape
    n = w.shape[2]
    return pl.pallas_call(
        _norm_matmul_kernel,
        grid=(m // tm, n // tn),
        in_specs=[
            pl.BlockSpec((tm, k), lambda i, j: (i, 0)),
            pl.BlockSpec((1, k), lambda i, j: (0, 0)),
            pl.BlockSpec((None, k, tn), lambda i, j: (idx, 0, j)),
        ],
        out_specs=pl.BlockSpec((tm, tn), lambda i, j: (i, j)),
        out_shape=jax.ShapeDtypeStruct((m, n), out_dtype),
        scratch_shapes=[pltpu.VMEM((tm, k), BF16)],
        compiler_params=_params("parallel", "arbitrary"),
        name="norm_matmul",
    )(x, g.reshape(1, k), w)


def _conv_kernel(a_ref, gate_ref, w_ref, b_ref, lg_ref, lb_ref, o_ref, pad_ref, *, seq, rows):
    zeros = jnp.zeros((CONV_HALO, CONV_GROUP_DIM), F32)
    pad_ref[0:CONV_HALO, :] = zeros
    pad_ref[CONV_HALO + seq:CONV_HALO + seq + CONV_HALO, :] = zeros
    pad_ref[CONV_HALO:CONV_HALO + seq, :] = a_ref[...] * jax.nn.sigmoid(gate_ref[...])
    w = w_ref[...]
    for c in range(seq // rows):
        base = c * rows + CONV_HALO - CONV_PAD
        acc = jnp.zeros((rows, CONV_GROUP_DIM), F32)
        for k in range(CONV_WIDTH):
            acc = acc + pad_ref[base + k:base + k + rows, :] * w[k:k + 1, :]
        y = _layer_norm(acc + b_ref[...], lg_ref[...], lb_ref[...])
        o_ref[c * rows:(c + 1) * rows, :] = _silu(y).astype(o_ref.dtype)


def _conv_module(z, dw_w, dw_b, ln_g, ln_b):
    bsz, seq, _ = z.shape
    cdim = dw_w.shape[1]
    ngroups = cdim // CONV_GROUP_DIM
    gd = CONV_GROUP_DIM
    vec = pl.BlockSpec((1, gd), lambda b, g: (0, g))
    return pl.pallas_call(
        functools.partial(_conv_kernel, seq=seq, rows=256),
        grid=(bsz, ngroups),
        in_specs=[
            pl.BlockSpec((None, seq, gd), lambda b, g: (b, 0, g)),
            pl.BlockSpec((None, seq, gd), lambda b, g: (b, 0, ngroups + g)),
            pl.BlockSpec((CONV_WIDTH, gd), lambda b, g: (0, g)),
            vec, vec, vec,
        ],
        out_specs=pl.BlockSpec((None, seq, gd), lambda b, g: (b, 0, g)),
        out_shape=jax.ShapeDtypeStruct((bsz, seq, cdim), BF16),
        scratch_shapes=[pltpu.VMEM((seq + 2 * CONV_HALO, gd), F32)],
        compiler_params=_params("parallel", "parallel"),
        name="conv_module",
    )(z, z, dw_w, dw_b.reshape(1, cdim), ln_g.reshape(1, cdim), ln_b.reshape(1, cdim))


def _sgu_kernel(zu_ref, zv_ref, lg_ref, lb_ref, ws_ref, bs_ref, o_ref, *, ts):
    hd = zu_ref.shape[-1] // SGU_HEADS
    for c in range(ts // CHUNK):
        rows = slice(c * CHUNK, (c + 1) * CHUNK)
        v = _layer_norm(_gelu(zv_ref[rows, :]), lg_ref[...], lb_ref[...]).astype(BF16)
        for h in range(SGU_HEADS):
            cols = slice(h * hd, (h + 1) * hd)
            mixed = jnp.dot(ws_ref[h].astype(BF16), v[:, cols], preferred_element_type=F32) + bs_ref[:, h:h + 1]
            o_ref[rows, cols] = (_gelu(zu_ref[rows, cols]) * mixed).astype(o_ref.dtype)


def _spatial_gating(z, ln_g, ln_b, w_s, b_s, *, ts):
    bsz, seq, in_dim = z.shape
    gdim = ln_g.shape[0]
    ublk = (in_dim - 2 * gdim) // gdim
    return pl.pallas_call(
        functools.partial(_sgu_kernel, ts=ts),
        grid=(bsz, seq // ts),
        in_specs=[
            pl.BlockSpec((None, ts, gdim), lambda b, t: (b, t, ublk)),
            pl.BlockSpec((None, ts, gdim), lambda b, t: (b, t, ublk + 1)),
            pl.BlockSpec((1, gdim), lambda b, t: (0, 0)),
            pl.BlockSpec((1, gdim), lambda b, t: (0, 0)),
            pl.BlockSpec((SGU_HEADS, CHUNK, CHUNK), lambda b, t: (0, 0, 0)),
            pl.BlockSpec((CHUNK, SGU_HEADS), lambda b, t: (0, 0)),
        ],
        out_specs=pl.BlockSpec((None, ts, gdim), lambda b, t: (b, t, 0)),
        out_shape=jax.ShapeDtypeStruct((bsz, seq, gdim), BF16),
        compiler_params=_params("parallel", "parallel"),
        name="spatial_gating",
    )(z, z, ln_g.reshape(1, gdim), ln_b.reshape(1, gdim), w_s, b_s.T)


def _proj_residual_kernel(*refs, n_lhs, has_bias):
    lhs_refs = refs[:n_lhs]
    w_ref, x_ref = refs[n_lhs], refs[n_lhs + 1]
    o_ref = refs[-1]
    acc = x_ref[...]
    if has_bias:
        acc = acc + refs[n_lhs + 2][...]
    k0 = 0
    for lhs_ref in lhs_refs:
        kw = lhs_ref.shape[-1]
        acc = acc + jnp.dot(lhs_ref[...], w_ref[k0:k0 + kw, :].astype(BF16), preferred_element_type=F32)
        k0 += kw
    o_ref[...] = acc


def _proj_residual(lhs_list, w, idx, x, bias=None, *, tm, tn):
    m, n = x.shape
    k = w.shape[1]
    in_specs = [pl.BlockSpec((tm, lhs.shape[1]), lambda i, j: (i, 0)) for lhs in lhs_list]
    in_specs += [pl.BlockSpec((None, k, tn), lambda i, j: (idx, 0, j)), pl.BlockSpec((tm, tn), lambda i, j: (i, j))]
    args = list(lhs_list) + [w, x]
    if bias is not None:
        in_specs.append(pl.BlockSpec((1, tn), lambda i, j: (0, j)))
        args.append(bias.reshape(1, n))
    return pl.pallas_call(
        functools.partial(_proj_residual_kernel, n_lhs=len(lhs_list), has_bias=bias is not None),
        grid=(m // tm, n // tn),
        in_specs=in_specs,
        out_specs=pl.BlockSpec((tm, tn), lambda i, j: (i, j)),
        out_shape=jax.ShapeDtypeStruct((m, n), F32),
        compiler_params=_params("parallel", "parallel"),
        name="proj_residual",
    )(*args)


def _ffn_kernel(x_ref, g_ref, wg_ref, wu_ref, wd_ref, fg_ref, o_ref, h_ref, *, final_norm):
    f = pl.program_id(1)

    @pl.when(f == 0)
    def _():
        x = x_ref[...]
        h_ref[...] = _rms(x, g_ref[...]).astype(BF16)
        o_ref[...] = x

    h = h_ref[...]
    gate = jnp.dot(h, wg_ref[...].astype(BF16), preferred_element_type=F32)
    up = jnp.dot(h, wu_ref[...].astype(BF16), preferred_element_type=F32)
    act = (_silu(gate) * up).astype(BF16)
    o_ref[...] += jnp.dot(act, wd_ref[...].astype(BF16), preferred_element_type=F32)

    if final_norm:
        @pl.when(f == pl.num_programs(1) - 1)
        def _():
            o_ref[...] = _rms(o_ref[...], fg_ref[...])


def _ffn(x, g, w_gate, w_up, w_down, idx, final_g, *, tm, tf, final_norm):
    m, d = x.shape
    dff = w_gate.shape[2]
    return pl.pallas_call(
        functools.partial(_ffn_kernel, final_norm=final_norm),
        grid=(m // tm, dff // tf),
        in_specs=[
            pl.BlockSpec((tm, d), lambda i, f: (i, 0)),
            pl.BlockSpec((1, d), lambda i, f: (0, 0)),
            pl.BlockSpec((None, d, tf), lambda i, f: (idx, 0, f)),
            pl.BlockSpec((None, d, tf), lambda i, f: (idx, 0, f)),
            pl.BlockSpec((None, tf, d), lambda i, f: (idx, f, 0)),
            pl.BlockSpec((1, d), lambda i, f: (0, 0)),
        ],
        out_specs=pl.BlockSpec((tm, d), lambda i, f: (i, 0)),
        out_shape=jax.ShapeDtypeStruct((m, d), F32),
        scratch_shapes=[pltpu.VMEM((tm, d), BF16)],
        compiler_params=_params("parallel", "arbitrary"),
        name="ffn",
    )(x, g.reshape(1, d), w_gate, w_up, w_down, final_g.reshape(1, d))


def _fnet_channel_kernel(x_ref, g_ref, cs_ref, a_ref, b_ref):
    h = _rms(x_ref[...], g_ref[...]).astype(BF16)
    gd = cs_ref.shape[0]
    for grp in range(FNET_GROUPS):
        cols = slice(grp * gd, (grp + 1) * gd)
        ab = jnp.dot(h[:, cols], cs_ref[...], preferred_element_type=F32)
        a_ref[:, cols] = ab[:, :gd].astype(a_ref.dtype)
        b_ref[:, cols] = ab[:, gd:].astype(b_ref.dtype)


def _fnet_channel(x, g, cs, *, tm):
    m, d = x.shape
    gd = cs.shape[0]
    out = jax.ShapeDtypeStruct((m, d), BF16)
    return pl.pallas_call(
        _fnet_channel_kernel,
        grid=(m // tm,),
        in_specs=[
            pl.BlockSpec((tm, d), lambda i: (i, 0)),
            pl.BlockSpec((1, d), lambda i: (0, 0)),
            pl.BlockSpec((gd, 2 * gd), lambda i: (0, 0)),
        ],
        out_specs=[pl.BlockSpec((tm, d), lambda i: (i, 0))] * 2,
        out_shape=[out, out],
        compiler_params=_params("parallel"),
        name="fnet_channel",
    )(x, g.reshape(1, d), cs)


def _fnet_seq_kernel(c_ref, s_ref, a_ref, b_ref, o_ref):
    y = jnp.dot(c_ref[...], a_ref[...], preferred_element_type=F32)
    y = y - jnp.dot(s_ref[...], b_ref[...], preferred_element_type=F32)
    o_ref[...] = y.astype(o_ref.dtype)


def _fnet_seq(cmat, smat, a, b, *, tm, tn):
    bsz, seq, d = a.shape
    return pl.pallas_call(
        _fnet_seq_kernel,
        grid=(bsz, d // tn, seq // tm),
        in_specs=[
            pl.BlockSpec((tm, seq), lambda bb, j, i: (i, 0)),
            pl.BlockSpec((tm, seq), lambda bb, j, i: (i, 0)),
            pl.BlockSpec((None, seq, tn), lambda bb, j, i: (bb, 0, j)),
            pl.BlockSpec((None, seq, tn), lambda bb, j, i: (bb, 0, j)),
        ],
        out_specs=pl.BlockSpec((None, tm, tn), lambda bb, j, i: (bb, i, j)),
        out_shape=jax.ShapeDtypeStruct((bsz, seq, d), BF16),
        compiler_params=_params("parallel", "parallel", "parallel"),
        name="fnet_seq",
    )(cmat, smat, a, b)


def _cos_sin(rows, cols, n):
    ang = ((rows[:, None] * cols[None, :]) % n).astype(F32) * (2.0 * math.pi / n)
    return jnp.cos(ang), jnp.sin(ang)


def _dft_tables(n):
    q = 1 << (int(math.log2(n)) // 2)
    cols = jnp.arange(n, dtype=jnp.int32)
    ca, sa = _cos_sin(jnp.arange(n // q, dtype=jnp.int32) * q, cols, n)
    cb, sb = _cos_sin(jnp.arange(q, dtype=jnp.int32), cols, n)
    scale = 1.0 / math.sqrt(n)
    ca, sa = (ca * scale)[:, None, :], (sa * scale)[:, None, :]
    cb, sb = cb[None, :, :], sb[None, :, :]
    return (ca * cb - sa * sb).reshape(n, n), (sa * cb + ca * sb).reshape(n, n)


def kernel(x, mix_norm_g, ffn_norm_g, final_norm_g, ab_w_in, conv_dw_w, conv_dw_b, conv_ln_g, conv_ln_b,
           sgu_ln_g, sgu_ln_b, sgu_w, sgu_b, ab_w_out, fnet_w_out, fnet_b_out, ffn_w_gate, ffn_w_up, ffn_w_down):
    bsz, seq, d = x.shape
    m = bsz * seq
    depth = mix_norm_g.shape[0]
    xf = x.reshape(m, d)
    for layer in range(depth):
        if layer % 2 == 0:
            i = layer // 2
            z = _norm_matmul(xf, mix_norm_g[layer], ab_w_in, i, tm=1024, tn=1024)
            z = z.reshape(bsz, seq, -1)
            y_conv = _conv_module(z, conv_dw_w[i], conv_dw_b[i], conv_ln_g[i], conv_ln_b[i])
            y_sgu = _spatial_gating(z, sgu_ln_g[i], sgu_ln_b[i], sgu_w[i], sgu_b[i], ts=512)
            xf = _proj_residual([y_conv.reshape(m, -1), y_sgu.reshape(m, -1)], ab_w_out, i, xf, tm=1024, tn=1024)
        else:
            j = layer // 2
            gd = d // FNET_GROUPS
            cc, sc = _dft_tables(gd)
            cs, ss = _dft_tables(seq)
            a, b = _fnet_channel(xf, mix_norm_g[layer], jnp.concatenate([cc, sc], axis=1).astype(BF16), tm=512)
            y = _fnet_seq(cs.astype(BF16), ss.astype(BF16), a.reshape(bsz, seq, d), b.reshape(bsz, seq, d),
                          tm=1024, tn=1024)
            xf = _proj_residual([y.reshape(m, d)], fnet_w_out, j, xf, fnet_b_out[j], tm=1024, tn=1024)
        xf = _ffn(xf, ffn_norm_g[layer], ffn_w_gate, ffn_w_up, ffn_w_down, layer, final_norm_g,
                  tm=1024, tf=256, final_norm=layer == depth - 1)
    return xf.reshape(bsz, seq, d)
```

```python
import functools
import math

import jax
import jax.numpy as jnp
from jax import lax
from jax.experimental import pallas as pl
from jax.experimental.pallas import tpu as pltpu

F32 = jnp.float32
BF16 = jnp.bfloat16

RMS_EPS = 1e-6
LN_EPS = 1e-5
CONV_GROUP_DIM = 128
CONV_WIDTH = 31
CONV_PAD = (CONV_WIDTH - 1) // 2
CONV_HALO = 16
SGU_HEADS = 8
CHUNK = 128
FNET_GROUPS = 8

V7X_VMEM_BYTES = 64 * 1024 * 1024
VMEM_LIMIT_BYTES = V7X_VMEM_BYTES - 8 * 1024 * 1024
ROW_CHUNK = 256


def _params(*semantics):
    return pltpu.CompilerParams(dimension_semantics=semantics, vmem_limit_bytes=VMEM_LIMIT_BYTES)


def _rms(x, g):
    return x * lax.rsqrt(jnp.mean(x * x, axis=-1, keepdims=True) + RMS_EPS) * g


def _layer_norm(x, g, b):
    mu = jnp.mean(x, axis=-1, keepdims=True)
    xc = x - mu
    return xc * lax.rsqrt(jnp.mean(xc * xc, axis=-1, keepdims=True) + LN_EPS) * g + b


def _gelu(x):
    return 0.5 * x * (1.0 + lax.erf(x * (1.0 / math.sqrt(2.0))))


def _silu(x):
    return x * jax.nn.sigmoid(x)


def _for_row_chunks(nrows, fn):
    def body(r, carry):
        fn(pl.ds(pl.multiple_of(r * ROW_CHUNK, ROW_CHUNK), ROW_CHUNK))
        return carry
    lax.fori_loop(0, nrows // ROW_CHUNK, body, 0)


def _cast_weight(w_ref, wb_ref):
    def cast(rows):
        wb_ref[rows, :] = w_ref[rows, :].astype(BF16)
    _for_row_chunks(w_ref.shape[0], cast)


def _conv_branch_kernel(x_ref, g_ref, wa_ref, wg_ref, dw_ref, b_ref, lg_ref, lb_ref, o_ref, h_ref, pad_ref, *, seq):
    gd = CONV_GROUP_DIM

    @pl.when(pl.program_id(1) == 0)
    def _():
        def norm(rows):
            h_ref[rows, :] = _rms(x_ref[rows, :], g_ref[...]).astype(BF16)
        _for_row_chunks(seq, norm)

    zeros = jnp.zeros((CONV_HALO, gd), F32)
    pad_ref[0:CONV_HALO, :] = zeros
    pad_ref[CONV_HALO + seq:CONV_HALO + seq + CONV_HALO, :] = zeros
    w = jnp.concatenate([wa_ref[...].astype(BF16), wg_ref[...].astype(BF16)], axis=1)
    dw = dw_ref[...]
    nchunks = seq // ROW_CHUNK

    def glu(c):
        z = jnp.dot(h_ref[c * ROW_CHUNK:(c + 1) * ROW_CHUNK, :], w, preferred_element_type=F32)
        start = CONV_HALO + c * ROW_CHUNK
        pad_ref[start:start + ROW_CHUNK, :] = z[:, :gd] * jax.nn.sigmoid(z[:, gd:])

    def conv(c):
        base = c * ROW_CHUNK + CONV_HALO - CONV_PAD
        acc = jnp.zeros((ROW_CHUNK, gd), F32)
        for k in range(CONV_WIDTH):
            acc = acc + pad_ref[base + k:base + k + ROW_CHUNK, :] * dw[k:k + 1, :]
        y = _layer_norm(acc + b_ref[...], lg_ref[...], lb_ref[...])
        o_ref[c * ROW_CHUNK:(c + 1) * ROW_CHUNK, :] = _silu(y).astype(o_ref.dtype)

    glu(0)
    for c in range(nchunks):
        if c + 1 < nchunks:
            glu(c + 1)
        conv(c)


def _conv_branch(x, g, w_in, idx, dw_w, dw_b, ln_g, ln_b):
    bsz, seq, d = x.shape
    cdim = dw_w.shape[1]
    gd = CONV_GROUP_DIM
    ngroups = cdim // gd
    vec = pl.BlockSpec((1, gd), lambda b, grp: (0, grp))
    return pl.pallas_call(
        functools.partial(_conv_branch_kernel, seq=seq),
        grid=(bsz, ngroups),
        in_specs=[
            pl.BlockSpec((None, seq, d), lambda b, grp: (b, 0, 0)),
            pl.BlockSpec((1, d), lambda b, grp: (0, 0)),
            pl.BlockSpec((None, d, gd), lambda b, grp: (idx, 0, grp)),
            pl.BlockSpec((None, d, gd), lambda b, grp: (idx, 0, ngroups + grp)),
            pl.BlockSpec((CONV_WIDTH, gd), lambda b, grp: (0, grp)),
            vec, vec, vec,
        ],
        out_specs=pl.BlockSpec((None, seq, gd), lambda b, grp: (b, 0, grp)),
        out_shape=jax.ShapeDtypeStruct((bsz, seq, cdim), BF16),
        scratch_shapes=[pltpu.VMEM((seq, d), BF16), pltpu.VMEM((seq + 2 * CONV_HALO, gd), F32)],
        compiler_params=_params("parallel", "arbitrary"),
        name="conv_branch",
    )(x, g.reshape(1, d), w_in, w_in, dw_w, dw_b.reshape(1, cdim), ln_g.reshape(1, cdim), ln_b.reshape(1, cdim))


def _sgu_branch_kernel(x_ref, g_ref, w_ref, lg_ref, lb_ref, ws_ref, bs_ref, o_ref, wb_ref, *, tm):
    @pl.when(pl.program_id(0) == 0)
    def _():
        _cast_weight(w_ref, wb_ref)

    gdim = o_ref.shape[-1]
    hd = gdim // SGU_HEADS
    for r in range(tm // ROW_CHUNK):
        r0 = r * ROW_CHUNK
        h = _rms(x_ref[r0:r0 + ROW_CHUNK, :], g_ref[...]).astype(BF16)
        z = jnp.dot(h, wb_ref[...], preferred_element_type=F32)
        for c in range(ROW_CHUNK // CHUNK):
            rows = slice(c * CHUNK, (c + 1) * CHUNK)
            v = _layer_norm(_gelu(z[rows, gdim:]), lg_ref[...], lb_ref[...]).astype(BF16)
            for hh in range(SGU_HEADS):
                cols = slice(hh * hd, (hh + 1) * hd)
                mixed = jnp.dot(ws_ref[hh].astype(BF16), v[:, cols], preferred_element_type=F32)
                mixed = mixed + bs_ref[:, hh:hh + 1]
                o_ref[r0 + c * CHUNK:r0 + (c + 1) * CHUNK, cols] = (_gelu(z[rows, cols]) * mixed).astype(o_ref.dtype)


def _sgu_branch(x, g, w_in, idx, ln_g, ln_b, w_s, b_s, *, tm):
    m, d = x.shape
    gdim = ln_g.shape[0]
    wblk = w_in.shape[2] // (2 * gdim) - 1
    return pl.pallas_call(
        functools.partial(_sgu_branch_kernel, tm=tm),
        grid=(m // tm,),
        in_specs=[
            pl.BlockSpec((tm, d), lambda i: (i, 0)),
            pl.BlockSpec((1, d), lambda i: (0, 0)),
            pl.BlockSpec((None, d, 2 * gdim), lambda i: (idx, 0, wblk), pipeline_mode=pl.Buffered(1)),
            pl.BlockSpec((1, gdim), lambda i: (0, 0)),
            pl.BlockSpec((1, gdim), lambda i: (0, 0)),
            pl.BlockSpec((SGU_HEADS, CHUNK, CHUNK), lambda i: (0, 0, 0)),
            pl.BlockSpec((CHUNK, SGU_HEADS), lambda i: (0, 0)),
        ],
        out_specs=pl.BlockSpec((tm, gdim), lambda i: (i, 0)),
        out_shape=jax.ShapeDtypeStruct((m, gdim), BF16),
        scratch_shapes=[pltpu.VMEM((d, 2 * gdim), BF16)],
        compiler_params=_params("arbitrary"),
        name="sgu_branch",
    )(x, g.reshape(1, d), w_in, ln_g.reshape(1, gdim), ln_b.reshape(1, gdim), w_s, b_s.T)


def _proj_residual_kernel(*refs, n_lhs, has_bias):
    lhs_refs = refs[:n_lhs]
    w_ref, x_ref = refs[n_lhs], refs[n_lhs + 1]
    o_ref, wb_ref = refs[-2], refs[-1]

    @pl.when(pl.program_id(0) == 0)
    def _():
        _cast_weight(w_ref, wb_ref)

    acc = x_ref[...]
    if has_bias:
        acc = acc + refs[n_lhs + 2][...]
    k0 = 0
    for lhs_ref in lhs_refs:
        kw = lhs_ref.shape[-1]
        acc = acc + jnp.dot(lhs_ref[...], wb_ref[k0:k0 + kw, :], preferred_element_type=F32)
        k0 += kw
    o_ref[...] = acc


def _proj_residual(lhs_list, w, idx, x, bias=None, *, tm):
    m, n = x.shape
    k = w.shape[1]
    in_specs = [pl.BlockSpec((tm, lhs.shape[1]), lambda i: (i, 0)) for lhs in lhs_list]
    in_specs += [pl.BlockSpec((None, k, n), lambda i: (idx, 0, 0), pipeline_mode=pl.Buffered(1)),
                 pl.BlockSpec((tm, n), lambda i: (i, 0))]
    args = list(lhs_list) + [w, x]
    if bias is not None:
        in_specs.append(pl.BlockSpec((1, n), lambda i: (0, 0)))
        args.append(bias.reshape(1, n))
    return pl.pallas_call(
        functools.partial(_proj_residual_kernel, n_lhs=len(lhs_list), has_bias=bias is not None),
        grid=(m // tm,),
        in_specs=in_specs,
        out_specs=pl.BlockSpec((tm, n), lambda i: (i, 0)),
        out_shape=jax.ShapeDtypeStruct((m, n), F32),
        scratch_shapes=[pltpu.VMEM((k, n), BF16)],
        compiler_params=_params("arbitrary"),
        name="proj_residual",
    )(*args)


def _ffn_kernel(x_ref, g_ref, wg_ref, wu_ref, wd_ref, fg_ref, o_ref, h_ref, *, final_norm):
    f = pl.program_id(1)

    @pl.when(f == 0)
    def _():
        x = x_ref[...]
        h_ref[...] = _rms(x, g_ref[...]).astype(BF16)
        o_ref[...] = x

    h = h_ref[...]
    gate = jnp.dot(h, wg_ref[...].astype(BF16), preferred_element_type=F32)
    up = jnp.dot(h, wu_ref[...].astype(BF16), preferred_element_type=F32)
    act = (_silu(gate) * up).astype(BF16)
    o_ref[...] += jnp.dot(act, wd_ref[...].astype(BF16), preferred_element_type=F32)

    if final_norm:
        @pl.when(f == pl.num_programs(1) - 1)
        def _():
            o_ref[...] = _rms(o_ref[...], fg_ref[...])


def _ffn(x, g, w_gate, w_up, w_down, idx, final_g, *, tm, tf, final_norm):
    m, d = x.shape
    dff = w_gate.shape[2]
    return pl.pallas_call(
        functools.partial(_ffn_kernel, final_norm=final_norm),
        grid=(m // tm, dff // tf),
        in_specs=[
            pl.BlockSpec((tm, d), lambda i, f: (i, 0)),
            pl.BlockSpec((1, d), lambda i, f: (0, 0)),
            pl.BlockSpec((None, d, tf), lambda i, f: (idx, 0, f)),
            pl.BlockSpec((None, d, tf), lambda i, f: (idx, 0, f)),
            pl.BlockSpec((None, tf, d), lambda i, f: (idx, f, 0)),
            pl.BlockSpec((1, d), lambda i, f: (0, 0)),
        ],
        out_specs=pl.BlockSpec((tm, d), lambda i, f: (i, 0)),
        out_shape=jax.ShapeDtypeStruct((m, d), F32),
        scratch_shapes=[pltpu.VMEM((tm, d), BF16)],
        compiler_params=_params("parallel", "arbitrary"),
        name="ffn",
    )(x, g.reshape(1, d), w_gate, w_up, w_down, final_g.reshape(1, d))


def _fnet_channel_kernel(x_ref, g_ref, cs_ref, a_ref, b_ref):
    h = _rms(x_ref[...], g_ref[...]).astype(BF16)
    gd = cs_ref.shape[0]
    for grp in range(FNET_GROUPS):
        cols = slice(grp * gd, (grp + 1) * gd)
        ab = jnp.dot(h[:, cols], cs_ref[...], preferred_element_type=F32)
        a_ref[:, cols] = ab[:, :gd].astype(a_ref.dtype)
        b_ref[:, cols] = ab[:, gd:].astype(b_ref.dtype)


def _fnet_channel(x, g, cs, *, tm):
    m, d = x.shape
    gd = cs.shape[0]
    out = jax.ShapeDtypeStruct((m, d), BF16)
    return pl.pallas_call(
        _fnet_channel_kernel,
        grid=(m // tm,),
        in_specs=[
            pl.BlockSpec((tm, d), lambda i: (i, 0)),
            pl.BlockSpec((1, d), lambda i: (0, 0)),
            pl.BlockSpec((gd, 2 * gd), lambda i: (0, 0)),
        ],
        out_specs=[pl.BlockSpec((tm, d), lambda i: (i, 0))] * 2,
        out_shape=[out, out],
        compiler_params=_params("parallel"),
        name="fnet_channel",
    )(x, g.reshape(1, d), cs)


def _fnet_seq_kernel(c_ref, s_ref, a_ref, b_ref, o_ref):
    y = jnp.dot(c_ref[...], a_ref[...], preferred_element_type=F32)
    y = y - jnp.dot(s_ref[...], b_ref[...], preferred_element_type=F32)
    o_ref[...] = y.astype(o_ref.dtype)


def _fnet_seq(cmat, smat, a, b, *, tm, tn):
    bsz, seq, d = a.shape
    return pl.pallas_call(
        _fnet_seq_kernel,
        grid=(bsz, d // tn, seq // tm),
        in_specs=[
            pl.BlockSpec((tm, seq), lambda bb, j, i: (i, 0)),
            pl.BlockSpec((tm, seq), lambda bb, j, i: (i, 0)),
            pl.BlockSpec((None, seq, tn), lambda bb, j, i: (bb, 0, j)),
            pl.BlockSpec((None, seq, tn), lambda bb, j, i: (bb, 0, j)),
        ],
        out_specs=pl.BlockSpec((None, tm, tn), lambda bb, j, i: (bb, i, j)),
        out_shape=jax.ShapeDtypeStruct((bsz, seq, d), BF16),
        compiler_params=_params("parallel", "parallel", "parallel"),
        name="fnet_seq",
    )(cmat, smat, a, b)


def _cos_sin(rows, cols, n):
    ang = ((rows[:, None] * cols[None, :]) % n).astype(F32) * (2.0 * math.pi / n)
    return jnp.cos(ang), jnp.sin(ang)


def _dft_tables(n):
    q = 1 << (int(math.log2(n)) // 2)
    cols = jnp.arange(n, dtype=jnp.int32)
    ca, sa = _cos_sin(jnp.arange(n // q, dtype=jnp.int32) * q, cols, n)
    cb, sb = _cos_sin(jnp.arange(q, dtype=jnp.int32), cols, n)
    scale = 1.0 / math.sqrt(n)
    ca, sa = (ca * scale)[:, None, :], (sa * scale)[:, None, :]
    cb, sb = cb[None, :, :], sb[None, :, :]
    return (ca * cb - sa * sb).reshape(n, n), (sa * cb + ca * sb).reshape(n, n)


def kernel(x, mix_norm_g, ffn_norm_g, final_norm_g, ab_w_in, conv_dw_w, conv_dw_b, conv_ln_g, conv_ln_b,
           sgu_ln_g, sgu_ln_b, sgu_w, sgu_b, ab_w_out, fnet_w_out, fnet_b_out, ffn_w_gate, ffn_w_up, ffn_w_down):
    bsz, seq, d = x.shape
    m = bsz * seq
    depth = mix_norm_g.shape[0]
    xf = x.reshape(m, d)
    for layer in range(depth):
        if layer % 2 == 0:
            i = layer // 2
            y_conv = _conv_branch(xf.reshape(bsz, seq, d), mix_norm_g[layer], ab_w_in, i,
                                  conv_dw_w[i], conv_dw_b[i], conv_ln_g[i], conv_ln_b[i])
            y_sgu = _sgu_branch(xf, mix_norm_g[layer], ab_w_in, i, sgu_ln_g[i], sgu_ln_b[i], sgu_w[i], sgu_b[i],
                                tm=512)
            xf = _proj_residual([y_conv.reshape(m, -1), y_sgu], ab_w_out, i, xf, tm=512)
        else:
            j = layer // 2
            gd = d // FNET_GROUPS
            cc, sc = _dft_tables(gd)
            cs, ss = _dft_tables(seq)
            a, b = _fnet_channel(xf, mix_norm_g[layer], jnp.concatenate([cc, sc], axis=1).astype(BF16), tm=512)
            y = _fnet_seq(cs.astype(BF16), ss.astype(BF16), a.reshape(bsz, seq, d), b.reshape(bsz, seq, d),
                          tm=1024, tn=1024)
            xf = _proj_residual([y.reshape(m, d)], fnet_w_out, j, xf, fnet_b_out[j], tm=512)
        xf = _ffn(xf, ffn_norm_g[layer], ffn_w_gate, ffn_w_up, ffn_w_down, layer, final_norm_g,
                  tm=1024, tf=256, final_norm=layer == depth - 1)
    return xf.reshape(bsz, seq, d)
```

```python
import functools
import math

import jax
import jax.numpy as jnp
from jax import lax
from jax.experimental import pallas as pl
from jax.experimental.pallas import tpu as pltpu

F32 = jnp.float32
BF16 = jnp.bfloat16

RMS_EPS = 1e-6
LN_EPS = 1e-5
CONV_GROUP_DIM = 128
CONV_WIDTH = 31
CONV_PAD = (CONV_WIDTH - 1) // 2
CONV_HALO = 16
CONV_ROWS = 64
CONV_BLOCK = 256
SUBLANES = 8
SGU_HEADS = 8
CHUNK = 128
FNET_GROUPS = 8

V7X_VMEM_BYTES = 64 * 1024 * 1024
VMEM_LIMIT_BYTES = V7X_VMEM_BYTES - 8 * 1024 * 1024
ROW_CHUNK = 256


def _params(*semantics, flags=None):
    return pltpu.CompilerParams(dimension_semantics=semantics, vmem_limit_bytes=VMEM_LIMIT_BYTES, flags=flags)


def _rms(x, g):
    return x * lax.rsqrt(jnp.mean(x * x, axis=-1, keepdims=True) + RMS_EPS) * g


def _layer_norm(x, g, b):
    mu = jnp.mean(x, axis=-1, keepdims=True)
    xc = x - mu
    return xc * lax.rsqrt(jnp.mean(xc * xc, axis=-1, keepdims=True) + LN_EPS) * g + b


def _gelu(x):
    return 0.5 * x * (1.0 + lax.erf(x * (1.0 / math.sqrt(2.0))))


def _silu(x):
    return x * jax.nn.sigmoid(x)


def _for_row_chunks(nrows, fn):
    def body(r, carry):
        fn(pl.ds(pl.multiple_of(r * ROW_CHUNK, ROW_CHUNK), ROW_CHUNK))
        return carry
    lax.fori_loop(0, nrows // ROW_CHUNK, body, 0)


def _cast_weight(w_ref, wb_ref):
    def cast(rows):
        wb_ref[rows, :] = w_ref[rows, :].astype(BF16)
    _for_row_chunks(w_ref.shape[0], cast)


def _sgu_branch_kernel(x_ref, g_ref, w_ref, lg_ref, lb_ref, ws_ref, bs_ref, o_ref, h_ref, wb_ref, *, tm):
    @pl.when(pl.program_id(0) == 0)
    def _():
        _cast_weight(w_ref, wb_ref)

    gdim = o_ref.shape[-1]
    hd = gdim // SGU_HEADS
    for r in range(tm // ROW_CHUNK):
        r0 = r * ROW_CHUNK
        h = _rms(x_ref[r0:r0 + ROW_CHUNK, :], g_ref[...]).astype(BF16)
        h_ref[r0:r0 + ROW_CHUNK, :] = h
        z = jnp.dot(h, wb_ref[...], preferred_element_type=F32)
        for c in range(ROW_CHUNK // CHUNK):
            rows = slice(c * CHUNK, (c + 1) * CHUNK)
            v = _layer_norm(_gelu(z[rows, gdim:]), lg_ref[...], lb_ref[...]).astype(BF16)
            for hh in range(SGU_HEADS):
                cols = slice(hh * hd, (hh + 1) * hd)
                mixed = jnp.dot(ws_ref[hh].astype(BF16), v[:, cols], preferred_element_type=F32)
                mixed = mixed + bs_ref[:, hh:hh + 1]
                o_ref[r0 + c * CHUNK:r0 + (c + 1) * CHUNK, cols] = (_gelu(z[rows, cols]) * mixed).astype(o_ref.dtype)


def _sgu_branch(x, g, w_in, idx, ln_g, ln_b, w_s, b_s, *, tm):
    m, d = x.shape
    gdim = ln_g.shape[0]
    wblk = w_in.shape[2] // (2 * gdim) - 1
    return pl.pallas_call(
        functools.partial(_sgu_branch_kernel, tm=tm),
        grid=(m // tm,),
        in_specs=[
            pl.BlockSpec((tm, d), lambda i: (i, 0)),
            pl.BlockSpec((1, d), lambda i: (0, 0)),
            pl.BlockSpec((None, d, 2 * gdim), lambda i: (idx, 0, wblk), pipeline_mode=pl.Buffered(1)),
            pl.BlockSpec((1, gdim), lambda i: (0, 0)),
            pl.BlockSpec((1, gdim), lambda i: (0, 0)),
            pl.BlockSpec((SGU_HEADS, CHUNK, CHUNK), lambda i: (0, 0, 0)),
            pl.BlockSpec((CHUNK, SGU_HEADS), lambda i: (0, 0)),
        ],
        out_specs=[pl.BlockSpec((tm, gdim), lambda i: (i, 0)), pl.BlockSpec((tm, d), lambda i: (i, 0))],
        out_shape=[jax.ShapeDtypeStruct((m, gdim), BF16), jax.ShapeDtypeStruct((m, d), BF16)],
        scratch_shapes=[pltpu.VMEM((d, 2 * gdim), BF16)],
        compiler_params=_params("arbitrary"),
        name="sgu_branch",
    )(x, g.reshape(1, d), w_in, ln_g.reshape(1, gdim), ln_b.reshape(1, gdim), w_s, b_s.T)


def _conv_step(h_ref, wa_ref, wg_ref, dwb_ref, b_ref, lg_ref, lb_ref, o_ref, stage_w, stage_r, *, seq):
    cw = o_ref.shape[-1]
    gd = CONV_GROUP_DIM
    w = jnp.concatenate([wa_ref[...].astype(BF16), wg_ref[...].astype(BF16)], axis=1)

    def glu(c):
        z = jnp.dot(h_ref[c * ROW_CHUNK:(c + 1) * ROW_CHUNK, :], w, preferred_element_type=F32)
        start = CONV_HALO + c * ROW_CHUNK
        u = z[:, :cw] * jax.nn.sigmoid(z[:, cw:])
        for grp in range(cw // gd):
            stage_w[grp, start:start + ROW_CHUNK, :] = u[:, grp * gd:(grp + 1) * gd]

    def conv(c):
        for s in range(ROW_CHUNK // CONV_ROWS):
            r0 = c * ROW_CHUNK + s * CONV_ROWS
            base = r0 + CONV_HALO - CONV_PAD
            for grp in range(cw // gd):
                cols = slice(grp * gd, (grp + 1) * gd)
                acc = jnp.zeros((CONV_ROWS // SUBLANES, SUBLANES, gd), F32)
                for k in range(CONV_WIDTH):
                    tap = stage_r[grp, base + k:base + k + CONV_ROWS, :].reshape(CONV_ROWS // SUBLANES, SUBLANES, gd)
                    acc = acc + tap * dwb_ref[k * SUBLANES:(k + 1) * SUBLANES, cols]
                y = _layer_norm(acc.reshape(CONV_ROWS, gd) + b_ref[:, cols], lg_ref[:, cols], lb_ref[:, cols])
                o_ref[r0:r0 + CONV_ROWS, cols] = _silu(y).astype(o_ref.dtype)

    for c in range(seq // ROW_CHUNK):
        glu(c)
        conv(c)


def _conv_branch_kernel(h_ref, wa_ref, wg_ref, dw_ref, b_ref, lg_ref, lb_ref, o_ref, stage_a, stage_b, dwb_ref, *, seq):
    t = pl.program_id(0)
    cw = o_ref.shape[-1]

    @pl.when(t == 0)
    def _():
        stage_a[...] = jnp.zeros_like(stage_a)
        stage_b[...] = jnp.zeros_like(stage_b)

    for k in range(CONV_WIDTH):
        dwb_ref[k * SUBLANES:(k + 1) * SUBLANES, :] = jnp.broadcast_to(dw_ref[k:k + 1, :], (SUBLANES, cw))

    step = functools.partial(_conv_step, h_ref, wa_ref, wg_ref, dwb_ref, b_ref, lg_ref, lb_ref, o_ref, seq=seq)

    @pl.when(t % 2 == 0)
    def _():
        step(stage_a, stage_b)

    @pl.when(t % 2 == 1)
    def _():
        step(stage_b, stage_a)


def _conv_branch(h, w_in, idx, dw_w, dw_b, ln_g, ln_b):
    bsz, seq, d = h.shape
    cdim = dw_w.shape[1]
    cw = CONV_BLOCK
    nblk = cdim // cw
    items = bsz * nblk

    def proj(t):
        return jnp.minimum(t, items - 1)

    def conv(t):
        return jnp.maximum(t - 1, 0)

    vec = pl.BlockSpec((1, cw), lambda t: (0, conv(t) % nblk))
    stage = (cw // CONV_GROUP_DIM, seq + 2 * CONV_HALO, CONV_GROUP_DIM)
    return pl.pallas_call(
        functools.partial(_conv_branch_kernel, seq=seq),
        grid=(items + 1,),
        in_specs=[
            pl.BlockSpec((None, seq, d), lambda t: (proj(t) // nblk, 0, 0)),
            pl.BlockSpec((None, d, cw), lambda t: (idx, 0, proj(t) % nblk)),
            pl.BlockSpec((None, d, cw), lambda t: (idx, 0, nblk + proj(t) % nblk)),
            pl.BlockSpec((CONV_WIDTH, cw), lambda t: (0, conv(t) % nblk)),
            vec, vec, vec,
        ],
        out_specs=pl.BlockSpec((None, seq, cw), lambda t: (conv(t) // nblk, 0, conv(t) % nblk)),
        out_shape=jax.ShapeDtypeStruct((bsz, seq, cdim), BF16),
        scratch_shapes=[pltpu.VMEM(stage, F32), pltpu.VMEM(stage, F32), pltpu.VMEM((CONV_WIDTH * SUBLANES, cw), F32)],
        compiler_params=_params("arbitrary"),
        name="conv_branch",
    )(h, w_in, w_in, dw_w, dw_b.reshape(1, cdim), ln_g.reshape(1, cdim), ln_b.reshape(1, cdim))


def _proj_residual_kernel(*refs, n_lhs, has_bias):
    lhs_refs = refs[:n_lhs]
    w_ref, x_ref = refs[n_lhs], refs[n_lhs + 1]
    o_ref, wb_ref = refs[-2], refs[-1]

    @pl.when(pl.program_id(0) == 0)
    def _():
        _cast_weight(w_ref, wb_ref)

    acc = x_ref[...]
    if has_bias:
        acc = acc + refs[n_lhs + 2][...]
    k0 = 0
    for lhs_ref in lhs_refs:
        kw = lhs_ref.shape[-1]
        acc = acc + jnp.dot(lhs_ref[...], wb_ref[k0:k0 + kw, :], preferred_element_type=F32)
        k0 += kw
    o_ref[...] = acc


def _proj_residual(lhs_list, w, idx, x, bias=None, *, tm):
    m, n = x.shape
    k = w.shape[1]
    in_specs = [pl.BlockSpec((tm, lhs.shape[1]), lambda i: (i, 0)) for lhs in lhs_list]
    in_specs += [pl.BlockSpec((None, k, n), lambda i: (idx, 0, 0), pipeline_mode=pl.Buffered(1)),
                 pl.BlockSpec((tm, n), lambda i: (i, 0))]
    args = list(lhs_list) + [w, x]
    if bias is not None:
        in_specs.append(pl.BlockSpec((1, n), lambda i: (0, 0)))
        args.append(bias.reshape(1, n))
    return pl.pallas_call(
        functools.partial(_proj_residual_kernel, n_lhs=len(lhs_list), has_bias=bias is not None),
        grid=(m // tm,),
        in_specs=in_specs,
        out_specs=pl.BlockSpec((tm, n), lambda i: (i, 0)),
        out_shape=jax.ShapeDtypeStruct((m, n), F32),
        scratch_shapes=[pltpu.VMEM((k, n), BF16)],
        compiler_params=_params("arbitrary"),
        name="proj_residual",
    )(*args)


def _ffn_kernel(x_ref, g_ref, wg_ref, wu_ref, wd_ref, fg_ref, o_ref, h_ref, *, final_norm):
    f = pl.program_id(1)

    @pl.when(f == 0)
    def _():
        x = x_ref[...]
        h_ref[...] = _rms(x, g_ref[...]).astype(BF16)
        o_ref[...] = x

    h = h_ref[...]
    gate = jnp.dot(h, wg_ref[...].astype(BF16), preferred_element_type=F32)
    up = jnp.dot(h, wu_ref[...].astype(BF16), preferred_element_type=F32)
    act = (_silu(gate) * up).astype(BF16)
    o_ref[...] += jnp.dot(act, wd_ref[...].astype(BF16), preferred_element_type=F32)

    if final_norm:
        @pl.when(f == pl.num_programs(1) - 1)
        def _():
            o_ref[...] = _rms(o_ref[...], fg_ref[...])


def _ffn(x, g, w_gate, w_up, w_down, idx, final_g, *, tm, tf, final_norm):
    m, d = x.shape
    dff = w_gate.shape[2]
    return pl.pallas_call(
        functools.partial(_ffn_kernel, final_norm=final_norm),
        grid=(m // tm, dff // tf),
        in_specs=[
            pl.BlockSpec((tm, d), lambda i, f: (i, 0)),
            pl.BlockSpec((1, d), lambda i, f: (0, 0)),
            pl.BlockSpec((None, d, tf), lambda i, f: (idx, 0, f)),
            pl.BlockSpec((None, d, tf), lambda i, f: (idx, 0, f)),
            pl.BlockSpec((None, tf, d), lambda i, f: (idx, f, 0)),
            pl.BlockSpec((1, d), lambda i, f: (0, 0)),
        ],
        out_specs=pl.BlockSpec((tm, d), lambda i, f: (i, 0)),
        out_shape=jax.ShapeDtypeStruct((m, d), F32),
        scratch_shapes=[pltpu.VMEM((tm, d), BF16)],
        compiler_params=_params("parallel", "arbitrary"),
        name="ffn",
    )(x, g.reshape(1, d), w_gate, w_up, w_down, final_g.reshape(1, d))


def _fnet_channel_kernel(x_ref, g_ref, cs_ref, a_ref, b_ref):
    h = _rms(x_ref[...], g_ref[...]).astype(BF16)
    gd = cs_ref.shape[0]
    for grp in range(FNET_GROUPS):
        cols = slice(grp * gd, (grp + 1) * gd)
        ab = jnp.dot(h[:, cols], cs_ref[...], preferred_element_type=F32)
        a_ref[:, cols] = ab[:, :gd].astype(a_ref.dtype)
        b_ref[:, cols] = ab[:, gd:].astype(b_ref.dtype)


def _fnet_channel(x, g, cs, *, tm):
    m, d = x.shape
    gd = cs.shape[0]
    out = jax.ShapeDtypeStruct((m, d), BF16)
    return pl.pallas_call(
        _fnet_channel_kernel,
        grid=(m // tm,),
        in_specs=[
            pl.BlockSpec((tm, d), lambda i: (i, 0)),
            pl.BlockSpec((1, d), lambda i: (0, 0)),
            pl.BlockSpec((gd, 2 * gd), lambda i: (0, 0)),
        ],
        out_specs=[pl.BlockSpec((tm, d), lambda i: (i, 0))] * 2,
        out_shape=[out, out],
        compiler_params=_params("parallel"),
        name="fnet_channel",
    )(x, g.reshape(1, d), cs)


def _fnet_seq_kernel(c_ref, s_ref, a_ref, b_ref, o_ref):
    y = jnp.dot(c_ref[...], a_ref[...], preferred_element_type=F32)
    y = y - jnp.dot(s_ref[...], b_ref[...], preferred_element_type=F32)
    o_ref[...] = y.astype(o_ref.dtype)


def _fnet_seq(cmat, smat, a, b, *, tm, tn):
    bsz, seq, d = a.shape
    return pl.pallas_call(
        _fnet_seq_kernel,
        grid=(bsz, d // tn, seq // tm),
        in_specs=[
            pl.BlockSpec((tm, seq), lambda bb, j, i: (i, 0)),
            pl.BlockSpec((tm, seq), lambda bb, j, i: (i, 0)),
            pl.BlockSpec((None, seq, tn), lambda bb, j, i: (bb, 0, j)),
            pl.BlockSpec((None, seq, tn), lambda bb, j, i: (bb, 0, j)),
        ],
        out_specs=pl.BlockSpec((None, tm, tn), lambda bb, j, i: (bb, i, j)),
        out_shape=jax.ShapeDtypeStruct((bsz, seq, d), BF16),
        compiler_params=_params("parallel", "parallel", "parallel"),
        name="fnet_seq",
    )(cmat, smat, a, b)


def _cos_sin(rows, cols, n):
    ang = ((rows[:, None] * cols[None, :]) % n).astype(F32) * (2.0 * math.pi / n)
    return jnp.cos(ang), jnp.sin(ang)


def _dft_tables(n):
    q = 1 << (int(math.log2(n)) // 2)
    cols = jnp.arange(n, dtype=jnp.int32)
    ca, sa = _cos_sin(jnp.arange(n // q, dtype=jnp.int32) * q, cols, n)
    cb, sb = _cos_sin(jnp.arange(q, dtype=jnp.int32), cols, n)
    scale = 1.0 / math.sqrt(n)
    ca, sa = (ca * scale)[:, None, :], (sa * scale)[:, None, :]
    cb, sb = cb[None, :, :], sb[None, :, :]
    return (ca * cb - sa * sb).reshape(n, n), (sa * cb + ca * sb).reshape(n, n)


def kernel(x, mix_norm_g, ffn_norm_g, final_norm_g, ab_w_in, conv_dw_w, conv_dw_b, conv_ln_g, conv_ln_b,
           sgu_ln_g, sgu_ln_b, sgu_w, sgu_b, ab_w_out, fnet_w_out, fnet_b_out, ffn_w_gate, ffn_w_up, ffn_w_down):
    bsz, seq, d = x.shape
    m = bsz * seq
    depth = mix_norm_g.shape[0]
    xf = x.reshape(m, d)
    for layer in range(depth):
        if layer % 2 == 0:
            i = layer // 2
            y_sgu, h = _sgu_branch(xf, mix_norm_g[layer], ab_w_in, i, sgu_ln_g[i], sgu_ln_b[i], sgu_w[i], sgu_b[i],
                                   tm=512)
            y_conv = _conv_branch(h.reshape(bsz, seq, d), ab_w_in, i, conv_dw_w[i], conv_dw_b[i], conv_ln_g[i],
                                  conv_ln_b[i])
            xf = _proj_residual([y_conv.reshape(m, -1), y_sgu], ab_w_out, i, xf, tm=512)
        else:
            j = layer // 2
            gd = d // FNET_GROUPS
            cc, sc = _dft_tables(gd)
            cs, ss = _dft_tables(seq)
            a, b = _fnet_channel(xf, mix_norm_g[layer], jnp.concatenate([cc, sc], axis=1).astype(BF16), tm=512)
            y = _fnet_seq(cs.astype(BF16), ss.astype(BF16), a.reshape(bsz, seq, d), b.reshape(bsz, seq, d),
                          tm=1024, tn=1024)
            xf = _proj_residual([y.reshape(m, d)], fnet_w_out, j, xf, fnet_b_out[j], tm=512)
        xf = _ffn(xf, ffn_norm_g[layer], ffn_w_gate, ffn_w_up, ffn_w_down, layer, final_norm_g,
                  tm=1024, tf=256, final_norm=layer == depth - 1)
    return xf.reshape(bsz, seq, d)
```

```python
import functools
import math

import jax
import jax.numpy as jnp
from jax import lax
from jax.experimental import pallas as pl
from jax.experimental.pallas import tpu as pltpu

F32 = jnp.float32
BF16 = jnp.bfloat16

RMS_EPS = 1e-6
LN_EPS = 1e-5
CONV_GROUP_DIM = 128
CONV_WIDTH = 31
CONV_PAD = (CONV_WIDTH - 1) // 2
CONV_HALO = 16
CONV_ROWS = 64
CONV_BLOCK = 256
SUBLANES = 8
SGU_HEADS = 8
CHUNK = 128
FNET_GROUPS = 8
FFT_RADIX = 4

V7X_VMEM_BYTES = 64 * 1024 * 1024
VMEM_LIMIT_BYTES = V7X_VMEM_BYTES - 8 * 1024 * 1024
ROW_CHUNK = 256


def _params(*semantics, flags=None):
    return pltpu.CompilerParams(dimension_semantics=semantics, vmem_limit_bytes=VMEM_LIMIT_BYTES, flags=flags)


def _rms(x, g):
    return x * lax.rsqrt(jnp.mean(x * x, axis=-1, keepdims=True) + RMS_EPS) * g


def _layer_norm(x, g, b):
    mu = jnp.mean(x, axis=-1, keepdims=True)
    xc = x - mu
    return xc * lax.rsqrt(jnp.mean(xc * xc, axis=-1, keepdims=True) + LN_EPS) * g + b


def _gelu(x):
    return 0.5 * x * (1.0 + lax.erf(x * (1.0 / math.sqrt(2.0))))


def _silu(x):
    return x * jax.nn.sigmoid(x)


def _for_row_chunks(nrows, fn):
    def body(r, carry):
        fn(pl.ds(pl.multiple_of(r * ROW_CHUNK, ROW_CHUNK), ROW_CHUNK))
        return carry
    lax.fori_loop(0, nrows // ROW_CHUNK, body, 0)


def _cast_weight(w_ref, wb_ref):
    def cast(rows):
        wb_ref[rows, :] = w_ref[rows, :].astype(BF16)
    _for_row_chunks(w_ref.shape[0], cast)


def _sgu_branch_kernel(x_ref, g_ref, w_ref, lg_ref, lb_ref, ws_ref, bs_ref, o_ref, h_ref, wb_ref, *, tm):
    @pl.when(pl.program_id(0) == 0)
    def _():
        _cast_weight(w_ref, wb_ref)

    gdim = o_ref.shape[-1]
    hd = gdim // SGU_HEADS
    for r in range(tm // ROW_CHUNK):
        r0 = r * ROW_CHUNK
        h = _rms(x_ref[r0:r0 + ROW_CHUNK, :], g_ref[...]).astype(BF16)
        h_ref[r0:r0 + ROW_CHUNK, :] = h
        z = jnp.dot(h, wb_ref[...], preferred_element_type=F32)
        for c in range(ROW_CHUNK // CHUNK):
            rows = slice(c * CHUNK, (c + 1) * CHUNK)
            v = _layer_norm(_gelu(z[rows, gdim:]), lg_ref[...], lb_ref[...]).astype(BF16)
            for hh in range(SGU_HEADS):
                cols = slice(hh * hd, (hh + 1) * hd)
                mixed = jnp.dot(ws_ref[hh].astype(BF16), v[:, cols], preferred_element_type=F32)
                mixed = mixed + bs_ref[:, hh:hh + 1]
                o_ref[r0 + c * CHUNK:r0 + (c + 1) * CHUNK, cols] = (_gelu(z[rows, cols]) * mixed).astype(o_ref.dtype)


def _sgu_branch(x, g, w_in, idx, ln_g, ln_b, w_s, b_s, *, tm):
    m, d = x.shape
    gdim = ln_g.shape[0]
    wblk = w_in.shape[2] // (2 * gdim) - 1
    return pl.pallas_call(
        functools.partial(_sgu_branch_kernel, tm=tm),
        grid=(m // tm,),
        in_specs=[
            pl.BlockSpec((tm, d), lambda i: (i, 0)),
            pl.BlockSpec((1, d), lambda i: (0, 0)),
            pl.BlockSpec((None, d, 2 * gdim), lambda i: (idx, 0, wblk), pipeline_mode=pl.Buffered(1)),
            pl.BlockSpec((1, gdim), lambda i: (0, 0)),
            pl.BlockSpec((1, gdim), lambda i: (0, 0)),
            pl.BlockSpec((SGU_HEADS, CHUNK, CHUNK), lambda i: (0, 0, 0)),
            pl.BlockSpec((CHUNK, SGU_HEADS), lambda i: (0, 0)),
        ],
        out_specs=[pl.BlockSpec((tm, gdim), lambda i: (i, 0)), pl.BlockSpec((tm, d), lambda i: (i, 0))],
        out_shape=[jax.ShapeDtypeStruct((m, gdim), BF16), jax.ShapeDtypeStruct((m, d), BF16)],
        scratch_shapes=[pltpu.VMEM((d, 2 * gdim), BF16)],
        compiler_params=_params("arbitrary"),
        name="sgu_branch",
    )(x, g.reshape(1, d), w_in, ln_g.reshape(1, gdim), ln_b.reshape(1, gdim), w_s, b_s.T)


def _conv_step(h_ref, wa_ref, wg_ref, dwb_ref, b_ref, lg_ref, lb_ref, o_ref, stage_w, stage_r, *, seq):
    cw = o_ref.shape[-1]
    gd = CONV_GROUP_DIM
    w = jnp.concatenate([wa_ref[...].astype(BF16), wg_ref[...].astype(BF16)], axis=1)

    def glu(c):
        z = jnp.dot(h_ref[c * ROW_CHUNK:(c + 1) * ROW_CHUNK, :], w, preferred_element_type=F32)
        start = CONV_HALO + c * ROW_CHUNK
        u = z[:, :cw] * jax.nn.sigmoid(z[:, cw:])
        for grp in range(cw // gd):
            stage_w[grp, start:start + ROW_CHUNK, :] = u[:, grp * gd:(grp + 1) * gd]

    def conv(c):
        for s in range(ROW_CHUNK // CONV_ROWS):
            r0 = c * ROW_CHUNK + s * CONV_ROWS
            base = r0 + CONV_HALO - CONV_PAD
            for grp in range(cw // gd):
                cols = slice(grp * gd, (grp + 1) * gd)
                acc = jnp.zeros((CONV_ROWS // SUBLANES, SUBLANES, gd), F32)
                for k in range(CONV_WIDTH):
                    tap = stage_r[grp, base + k:base + k + CONV_ROWS, :].reshape(CONV_ROWS // SUBLANES, SUBLANES, gd)
                    acc = acc + tap * dwb_ref[k * SUBLANES:(k + 1) * SUBLANES, cols]
                y = _layer_norm(acc.reshape(CONV_ROWS, gd) + b_ref[:, cols], lg_ref[:, cols], lb_ref[:, cols])
                o_ref[r0:r0 + CONV_ROWS, cols] = _silu(y).astype(o_ref.dtype)

    for c in range(seq // ROW_CHUNK):
        glu(c)
        conv(c)


def _conv_branch_kernel(h_ref, wa_ref, wg_ref, dw_ref, b_ref, lg_ref, lb_ref, o_ref, stage_a, stage_b, dwb_ref, *, seq):
    t = pl.program_id(0)
    cw = o_ref.shape[-1]

    @pl.when(t == 0)
    def _():
        stage_a[...] = jnp.zeros_like(stage_a)
        stage_b[...] = jnp.zeros_like(stage_b)

    for k in range(CONV_WIDTH):
        dwb_ref[k * SUBLANES:(k + 1) * SUBLANES, :] = jnp.broadcast_to(dw_ref[k:k + 1, :], (SUBLANES, cw))

    step = functools.partial(_conv_step, h_ref, wa_ref, wg_ref, dwb_ref, b_ref, lg_ref, lb_ref, o_ref, seq=seq)

    @pl.when(t % 2 == 0)
    def _():
        step(stage_a, stage_b)

    @pl.when(t % 2 == 1)
    def _():
        step(stage_b, stage_a)


def _conv_branch(h, w_in, idx, dw_w, dw_b, ln_g, ln_b):
    bsz, seq, d = h.shape
    cdim = dw_w.shape[1]
    cw = CONV_BLOCK
    nblk = cdim // cw
    items = bsz * nblk

    def proj(t):
        return jnp.minimum(t, items - 1)

    def conv(t):
        return jnp.maximum(t - 1, 0)

    vec = pl.BlockSpec((1, cw), lambda t: (0, conv(t) % nblk))
    stage = (cw // CONV_GROUP_DIM, seq + 2 * CONV_HALO, CONV_GROUP_DIM)
    return pl.pallas_call(
        functools.partial(_conv_branch_kernel, seq=seq),
        grid=(items + 1,),
        in_specs=[
            pl.BlockSpec((None, seq, d), lambda t: (proj(t) // nblk, 0, 0)),
            pl.BlockSpec((None, d, cw), lambda t: (idx, 0, proj(t) % nblk)),
            pl.BlockSpec((None, d, cw), lambda t: (idx, 0, nblk + proj(t) % nblk)),
            pl.BlockSpec((CONV_WIDTH, cw), lambda t: (0, conv(t) % nblk)),
            vec, vec, vec,
        ],
        out_specs=pl.BlockSpec((None, seq, cw), lambda t: (conv(t) // nblk, 0, conv(t) % nblk)),
        out_shape=jax.ShapeDtypeStruct((bsz, seq, cdim), BF16),
        scratch_shapes=[pltpu.VMEM(stage, F32), pltpu.VMEM(stage, F32), pltpu.VMEM((CONV_WIDTH * SUBLANES, cw), F32)],
        compiler_params=_params("arbitrary"),
        name="conv_branch",
    )(h, w_in, w_in, dw_w, dw_b.reshape(1, cdim), ln_g.reshape(1, cdim), ln_b.reshape(1, cdim))


def _proj_residual_kernel(*refs, n_lhs, has_bias):
    lhs_refs = refs[:n_lhs]
    w_ref, x_ref = refs[n_lhs], refs[n_lhs + 1]
    o_ref, wb_ref = refs[-2], refs[-1]

    @pl.when(pl.program_id(0) == 0)
    def _():
        _cast_weight(w_ref, wb_ref)

    acc = x_ref[...]
    if has_bias:
        acc = acc + refs[n_lhs + 2][...]
    k0 = 0
    for lhs_ref in lhs_refs:
        kw = lhs_ref.shape[-1]
        acc = acc + jnp.dot(lhs_ref[...], wb_ref[k0:k0 + kw, :], preferred_element_type=F32)
        k0 += kw
    o_ref[...] = acc


def _proj_residual(lhs_list, w, idx, x, bias=None, *, tm, residue_major=None):
    m, n = x.shape
    k = w.shape[1]
    if residue_major is None:
        x_spec = pl.BlockSpec((tm, n), lambda i: (i, 0))
        x_arg, out_shape = x, (m, n)
    else:
        r = residue_major
        x_spec = pl.BlockSpec((None, tm, n), lambda i: (i // r, 0, i % r))
        out_shape = (m // (r * tm), tm, r * n)
        x_arg = x.reshape(out_shape)
    in_specs = [pl.BlockSpec((tm, lhs.shape[1]), lambda i: (i, 0)) for lhs in lhs_list]
    in_specs += [pl.BlockSpec((None, k, n), lambda i: (idx, 0, 0), pipeline_mode=pl.Buffered(1)), x_spec]
    args = list(lhs_list) + [w, x_arg]
    if bias is not None:
        in_specs.append(pl.BlockSpec((1, n), lambda i: (0, 0)))
        args.append(bias.reshape(1, n))
    out = pl.pallas_call(
        functools.partial(_proj_residual_kernel, n_lhs=len(lhs_list), has_bias=bias is not None),
        grid=(m // tm,),
        in_specs=in_specs,
        out_specs=x_spec,
        out_shape=jax.ShapeDtypeStruct(out_shape, F32),
        scratch_shapes=[pltpu.VMEM((k, n), BF16)],
        compiler_params=_params("arbitrary"),
        name="proj_residual",
    )(*args)
    return out.reshape(m, n)


def _ffn_kernel(x_ref, g_ref, wg_ref, wu_ref, wd_ref, fg_ref, o_ref, h_ref, *, final_norm):
    f = pl.program_id(1)

    @pl.when(f == 0)
    def _():
        x = x_ref[...]
        h_ref[...] = _rms(x, g_ref[...]).astype(BF16)
        o_ref[...] = x

    h = h_ref[...]
    gate = jnp.dot(h, wg_ref[...].astype(BF16), preferred_element_type=F32)
    up = jnp.dot(h, wu_ref[...].astype(BF16), preferred_element_type=F32)
    act = (_silu(gate) * up).astype(BF16)
    o_ref[...] += jnp.dot(act, wd_ref[...].astype(BF16), preferred_element_type=F32)

    if final_norm:
        @pl.when(f == pl.num_programs(1) - 1)
        def _():
            o_ref[...] = _rms(o_ref[...], fg_ref[...])


def _ffn(x, g, w_gate, w_up, w_down, idx, final_g, *, tm, tf, final_norm):
    m, d = x.shape
    dff = w_gate.shape[2]
    return pl.pallas_call(
        functools.partial(_ffn_kernel, final_norm=final_norm),
        grid=(m // tm, dff // tf),
        in_specs=[
            pl.BlockSpec((tm, d), lambda i, f: (i, 0)),
            pl.BlockSpec((1, d), lambda i, f: (0, 0)),
            pl.BlockSpec((None, d, tf), lambda i, f: (idx, 0, f)),
            pl.BlockSpec((None, d, tf), lambda i, f: (idx, 0, f)),
            pl.BlockSpec((None, tf, d), lambda i, f: (idx, f, 0)),
            pl.BlockSpec((1, d), lambda i, f: (0, 0)),
        ],
        out_specs=pl.BlockSpec((tm, d), lambda i, f: (i, 0)),
        out_shape=jax.ShapeDtypeStruct((m, d), F32),
        scratch_shapes=[pltpu.VMEM((tm, d), BF16)],
        compiler_params=_params("parallel", "arbitrary"),
        name="ffn",
    )(x, g.reshape(1, d), w_gate, w_up, w_down, final_g.reshape(1, d))


def _fnet_channel_kernel(x_ref, g_ref, cs_ref, a_ref, b_ref):
    h = _rms(x_ref[...], g_ref[...]).astype(BF16)
    gd = cs_ref.shape[0]
    for grp in range(FNET_GROUPS):
        cols = slice(grp * gd, (grp + 1) * gd)
        ab = jnp.dot(h[:, cols], cs_ref[...], preferred_element_type=F32)
        a_ref[:, cols] = ab[:, :gd].astype(a_ref.dtype)
        b_ref[:, cols] = ab[:, gd:].astype(b_ref.dtype)


def _fnet_channel(x, g, cs, *, tm):
    m, d = x.shape
    gd = cs.shape[0]
    out = jax.ShapeDtypeStruct((m, d), BF16)
    return pl.pallas_call(
        _fnet_channel_kernel,
        grid=(m // tm,),
        in_specs=[
            pl.BlockSpec((tm, d), lambda i: (i, 0)),
            pl.BlockSpec((1, d), lambda i: (0, 0)),
            pl.BlockSpec((gd, 2 * gd), lambda i: (0, 0)),
        ],
        out_specs=[pl.BlockSpec((tm, d), lambda i: (i, 0))] * 2,
        out_shape=[out, out],
        compiler_params=_params("parallel"),
        name="fnet_channel",
    )(x, g.reshape(1, d), cs)


def _fnet_seq_kernel(t_ref, a_ref, b_ref, o_ref, uv_ref):
    q = o_ref.shape[1]

    def fold(rows):
        a0, a1, a2, a3 = [a_ref[pl.ds(m * q + rows.start, rows.size), :].astype(F32) for m in range(FFT_RADIX)]
        b0, b1, b2, b3 = [b_ref[pl.ds(m * q + rows.start, rows.size), :].astype(F32) for m in range(FFT_RADIX)]
        ae, ao, ad, aq = a0 + a2, a1 + a3, a0 - a2, a1 - a3
        be, bo, bd, bq = b0 + b2, b1 + b3, b0 - b2, b1 - b3
        u = (ae + ao, ad - bq, ae - ao, ad + bq)
        v = (be + bo, bd + aq, be - bo, bd - aq)
        for r in range(FFT_RADIX):
            uv_ref[r, rows, :] = u[r].astype(BF16)
            uv_ref[r, pl.ds(q + rows.start, rows.size), :] = v[r].astype(BF16)

    _for_row_chunks(q, fold)
    for r in range(FFT_RADIX):
        o_ref[r] = jnp.dot(t_ref[r], uv_ref[r], preferred_element_type=F32).astype(o_ref.dtype)


def _fnet_seq(tables, a, b, *, tn):
    bsz, seq, d = a.shape
    q = seq // FFT_RADIX
    return pl.pallas_call(
        _fnet_seq_kernel,
        grid=(bsz, d // tn),
        in_specs=[
            pl.BlockSpec((FFT_RADIX, q, 2 * q), lambda bb, j: (0, 0, 0)),
            pl.BlockSpec((None, seq, tn), lambda bb, j: (bb, 0, j)),
            pl.BlockSpec((None, seq, tn), lambda bb, j: (bb, 0, j)),
        ],
        out_specs=pl.BlockSpec((None, FFT_RADIX, q, tn), lambda bb, j: (bb, 0, 0, j)),
        out_shape=jax.ShapeDtypeStruct((bsz, FFT_RADIX, q, d), BF16),
        scratch_shapes=[pltpu.VMEM((FFT_RADIX, 2 * q, tn), BF16)],
        compiler_params=_params("parallel", "parallel"),
        name="fnet_seq",
    )(tables, a, b)


def _cos_sin(rows, cols, n):
    ang = ((rows[:, None] * cols[None, :]) % n).astype(F32) * (2.0 * math.pi / n)
    return jnp.cos(ang), jnp.sin(ang)


def _dft_tables(n, scale):
    q = 1 << (int(math.log2(n)) // 2)
    cols = jnp.arange(n, dtype=jnp.int32)
    ca, sa = _cos_sin(jnp.arange(n // q, dtype=jnp.int32) * q, cols, n)
    cb, sb = _cos_sin(jnp.arange(q, dtype=jnp.int32), cols, n)
    ca, sa = (ca * scale)[:, None, :], (sa * scale)[:, None, :]
    cb, sb = cb[None, :, :], sb[None, :, :]
    return (ca * cb - sa * sb).reshape(n, n), (sa * cb + ca * sb).reshape(n, n)


def _radix_tables(n):
    q = n // FFT_RADIX
    c4, s4 = _dft_tables(q, 1.0 / math.sqrt(n))
    cr, sr = _cos_sin(jnp.arange(FFT_RADIX, dtype=jnp.int32), jnp.arange(q, dtype=jnp.int32), n)
    cr, sr = cr[:, None, :], sr[:, None, :]
    c = c4[None] * cr - s4[None] * sr
    sn = s4[None] * cr + c4[None] * sr
    return jnp.concatenate([c, -sn], axis=2)


def kernel(x, mix_norm_g, ffn_norm_g, final_norm_g, ab_w_in, conv_dw_w, conv_dw_b, conv_ln_g, conv_ln_b,
           sgu_ln_g, sgu_ln_b, sgu_w, sgu_b, ab_w_out, fnet_w_out, fnet_b_out, ffn_w_gate, ffn_w_up, ffn_w_down):
    bsz, seq, d = x.shape
    m = bsz * seq
    depth = mix_norm_g.shape[0]
    xf = x.reshape(m, d)
    for layer in range(depth):
        if layer % 2 == 0:
            i = layer // 2
            y_sgu, h = _sgu_branch(xf, mix_norm_g[layer], ab_w_in, i, sgu_ln_g[i], sgu_ln_b[i], sgu_w[i], sgu_b[i],
                                   tm=512)
            y_conv = _conv_branch(h.reshape(bsz, seq, d), ab_w_in, i, conv_dw_w[i], conv_dw_b[i], conv_ln_g[i],
                                  conv_ln_b[i])
            xf = _proj_residual([y_conv.reshape(m, -1), y_sgu], ab_w_out, i, xf, tm=512)
        else:
            j = layer // 2
            gd = d // FNET_GROUPS
            cc, sc = _dft_tables(gd, 1.0 / math.sqrt(gd))
            a, b = _fnet_channel(xf, mix_norm_g[layer], jnp.concatenate([cc, sc], axis=1).astype(BF16), tm=512)
            y = _fnet_seq(_radix_tables(seq).astype(BF16), a.reshape(bsz, seq, d), b.reshape(bsz, seq, d), tn=1024)
            xf = _proj_residual([y.reshape(m, d)], fnet_w_out, j, xf, fnet_b_out[j], tm=seq // FFT_RADIX,
                                residue_major=FFT_RADIX)
        xf = _ffn(xf, ffn_norm_g[layer], ffn_w_gate, ffn_w_up, ffn_w_down, layer, final_norm_g,
                  tm=1024, tf=256, final_norm=layer == depth - 1)
    return xf.reshape(bsz, seq, d)
```

```python
import functools
import math

import jax
import jax.numpy as jnp
from jax import lax
from jax.experimental import pallas as pl
from jax.experimental.pallas import tpu as pltpu

F32 = jnp.float32
BF16 = jnp.bfloat16

RMS_EPS = 1e-6
LN_EPS = 1e-5
CONV_GROUP_DIM = 128
CONV_WIDTH = 31
CONV_PAD = (CONV_WIDTH - 1) // 2
CONV_HALO = 16
CONV_ROWS = 64
CONV_BLOCK = 256
SUBLANES = 8
SGU_HEADS = 8
CHUNK = 128
FNET_GROUPS = 8
FFT_RADIX = 4

V7X_VMEM_BYTES = 64 * 1024 * 1024
VMEM_LIMIT_BYTES = V7X_VMEM_BYTES - 8 * 1024 * 1024
ROW_CHUNK = 256


def _params(*semantics, flags=None):
    return pltpu.CompilerParams(dimension_semantics=semantics, vmem_limit_bytes=VMEM_LIMIT_BYTES, flags=flags)


def _rms(x, g):
    return x * lax.rsqrt(jnp.mean(x * x, axis=-1, keepdims=True) + RMS_EPS) * g


def _layer_norm(x, g, b):
    mu = jnp.mean(x, axis=-1, keepdims=True)
    xc = x - mu
    return xc * lax.rsqrt(jnp.mean(xc * xc, axis=-1, keepdims=True) + LN_EPS) * g + b


def _gelu(x):
    return 0.5 * x * (1.0 + lax.erf(x * (1.0 / math.sqrt(2.0))))


def _silu(x):
    return x * jax.nn.sigmoid(x)


def _for_row_chunks(nrows, fn):
    def body(r, carry):
        fn(pl.ds(pl.multiple_of(r * ROW_CHUNK, ROW_CHUNK), ROW_CHUNK))
        return carry
    lax.fori_loop(0, nrows // ROW_CHUNK, body, 0)


def _cast_weight(w_ref, wb_ref):
    def cast(rows):
        wb_ref[rows, :] = w_ref[rows, :].astype(BF16)
    _for_row_chunks(w_ref.shape[0], cast)


def _sgu_branch_kernel(x_ref, g_ref, w_ref, lg_ref, lb_ref, ws_ref, bs_ref, o_ref, h_ref, wb_ref, *, tm):
    @pl.when(pl.program_id(0) == 0)
    def _():
        _cast_weight(w_ref, wb_ref)

    gdim = o_ref.shape[-1]
    hd = gdim // SGU_HEADS
    for r in range(tm // ROW_CHUNK):
        r0 = r * ROW_CHUNK
        h = _rms(x_ref[r0:r0 + ROW_CHUNK, :], g_ref[...]).astype(BF16)
        h_ref[r0:r0 + ROW_CHUNK, :] = h
        z = jnp.dot(h, wb_ref[...], preferred_element_type=F32)
        for c in range(ROW_CHUNK // CHUNK):
            rows = slice(c * CHUNK, (c + 1) * CHUNK)
            v = _layer_norm(_gelu(z[rows, gdim:]), lg_ref[...], lb_ref[...]).astype(BF16)
            for hh in range(SGU_HEADS):
                cols = slice(hh * hd, (hh + 1) * hd)
                mixed = jnp.dot(ws_ref[hh].astype(BF16), v[:, cols], preferred_element_type=F32)
                mixed = mixed + bs_ref[:, hh:hh + 1]
                o_ref[r0 + c * CHUNK:r0 + (c + 1) * CHUNK, cols] = (_gelu(z[rows, cols]) * mixed).astype(o_ref.dtype)


def _sgu_branch(x, g, w_in, idx, ln_g, ln_b, w_s, b_s, *, tm):
    m, d = x.shape
    gdim = ln_g.shape[0]
    wblk = w_in.shape[2] // (2 * gdim) - 1
    return pl.pallas_call(
        functools.partial(_sgu_branch_kernel, tm=tm),
        grid=(m // tm,),
        in_specs=[
            pl.BlockSpec((tm, d), lambda i: (i, 0)),
            pl.BlockSpec((1, d), lambda i: (0, 0)),
            pl.BlockSpec((None, d, 2 * gdim), lambda i: (idx, 0, wblk), pipeline_mode=pl.Buffered(1)),
            pl.BlockSpec((1, gdim), lambda i: (0, 0)),
            pl.BlockSpec((1, gdim), lambda i: (0, 0)),
            pl.BlockSpec((SGU_HEADS, CHUNK, CHUNK), lambda i: (0, 0, 0)),
            pl.BlockSpec((CHUNK, SGU_HEADS), lambda i: (0, 0)),
        ],
        out_specs=[pl.BlockSpec((tm, gdim), lambda i: (i, 0)), pl.BlockSpec((tm, d), lambda i: (i, 0))],
        out_shape=[jax.ShapeDtypeStruct((m, gdim), BF16), jax.ShapeDtypeStruct((m, d), BF16)],
        scratch_shapes=[pltpu.VMEM((d, 2 * gdim), BF16)],
        compiler_params=_params("arbitrary"),
        name="sgu_branch",
    )(x, g.reshape(1, d), w_in, ln_g.reshape(1, gdim), ln_b.reshape(1, gdim), w_s, b_s.T)


def _conv_step(h_ref, wa_ref, wg_ref, dwb_ref, b_ref, lg_ref, lb_ref, o_ref, stage_w, stage_r, *, seq):
    cw = o_ref.shape[-1]
    gd = CONV_GROUP_DIM
    w = jnp.concatenate([wa_ref[...].astype(BF16), wg_ref[...].astype(BF16)], axis=1)

    def glu(c):
        z = jnp.dot(h_ref[c * ROW_CHUNK:(c + 1) * ROW_CHUNK, :], w, preferred_element_type=F32)
        start = CONV_HALO + c * ROW_CHUNK
        u = z[:, :cw] * jax.nn.sigmoid(z[:, cw:])
        for grp in range(cw // gd):
            stage_w[grp, start:start + ROW_CHUNK, :] = u[:, grp * gd:(grp + 1) * gd]

    def conv(c):
        for s in range(ROW_CHUNK // CONV_ROWS):
            r0 = c * ROW_CHUNK + s * CONV_ROWS
            base = r0 + CONV_HALO - CONV_PAD
            for grp in range(cw // gd):
                cols = slice(grp * gd, (grp + 1) * gd)
                acc = jnp.zeros((CONV_ROWS // SUBLANES, SUBLANES, gd), F32)
                for k in range(CONV_WIDTH):
                    tap = stage_r[grp, base + k:base + k + CONV_ROWS, :].reshape(CONV_ROWS // SUBLANES, SUBLANES, gd)
                    acc = acc + tap * dwb_ref[k * SUBLANES:(k + 1) * SUBLANES, cols]
                y = _layer_norm(acc.reshape(CONV_ROWS, gd) + b_ref[:, cols], lg_ref[:, cols], lb_ref[:, cols])
                o_ref[r0:r0 + CONV_ROWS, cols] = _silu(y).astype(o_ref.dtype)

    for c in range(seq // ROW_CHUNK):
        glu(c)
        conv(c)


def _conv_branch_kernel(h_ref, wa_ref, wg_ref, dw_ref, b_ref, lg_ref, lb_ref, o_ref, stage_a, stage_b, dwb_ref, *, seq):
    t = pl.program_id(0)
    cw = o_ref.shape[-1]

    @pl.when(t == 0)
    def _():
        stage_a[...] = jnp.zeros_like(stage_a)
        stage_b[...] = jnp.zeros_like(stage_b)

    for k in range(CONV_WIDTH):
        dwb_ref[k * SUBLANES:(k + 1) * SUBLANES, :] = jnp.broadcast_to(dw_ref[k:k + 1, :], (SUBLANES, cw))

    step = functools.partial(_conv_step, h_ref, wa_ref, wg_ref, dwb_ref, b_ref, lg_ref, lb_ref, o_ref, seq=seq)

    @pl.when(t % 2 == 0)
    def _():
        step(stage_a, stage_b)

    @pl.when(t % 2 == 1)
    def _():
        step(stage_b, stage_a)


def _conv_branch(h, w_in, idx, dw_w, dw_b, ln_g, ln_b):
    bsz, seq, d = h.shape
    cdim = dw_w.shape[1]
    cw = CONV_BLOCK
    nblk = cdim // cw
    items = bsz * nblk

    def proj(t):
        return jnp.minimum(t, items - 1)

    def conv(t):
        return jnp.maximum(t - 1, 0)

    vec = pl.BlockSpec((1, cw), lambda t: (0, conv(t) % nblk))
    stage = (cw // CONV_GROUP_DIM, seq + 2 * CONV_HALO, CONV_GROUP_DIM)
    return pl.pallas_call(
        functools.partial(_conv_branch_kernel, seq=seq),
        grid=(items + 1,),
        in_specs=[
            pl.BlockSpec((None, seq, d), lambda t: (proj(t) // nblk, 0, 0)),
            pl.BlockSpec((None, d, cw), lambda t: (idx, 0, proj(t) % nblk)),
            pl.BlockSpec((None, d, cw), lambda t: (idx, 0, nblk + proj(t) % nblk)),
            pl.BlockSpec((CONV_WIDTH, cw), lambda t: (0, conv(t) % nblk)),
            vec, vec, vec,
        ],
        out_specs=pl.BlockSpec((None, seq, cw), lambda t: (conv(t) // nblk, 0, conv(t) % nblk)),
        out_shape=jax.ShapeDtypeStruct((bsz, seq, cdim), BF16),
        scratch_shapes=[pltpu.VMEM(stage, F32), pltpu.VMEM(stage, F32), pltpu.VMEM((CONV_WIDTH * SUBLANES, cw), F32)],
        compiler_params=_params("arbitrary"),
        name="conv_branch",
    )(h, w_in, w_in, dw_w, dw_b.reshape(1, cdim), ln_g.reshape(1, cdim), ln_b.reshape(1, cdim))


def _proj_residual_kernel(*refs, n_lhs, has_bias):
    lhs_refs = refs[:n_lhs]
    w_ref, x_ref = refs[n_lhs], refs[n_lhs + 1]
    o_ref, wb_ref = refs[-2], refs[-1]

    @pl.when(pl.program_id(0) == 0)
    def _():
        _cast_weight(w_ref, wb_ref)

    acc = x_ref[...]
    if has_bias:
        acc = acc + refs[n_lhs + 2][...]
    k0 = 0
    for lhs_ref in lhs_refs:
        kw = lhs_ref.shape[-1]
        acc = acc + jnp.dot(lhs_ref[...], wb_ref[k0:k0 + kw, :], preferred_element_type=F32)
        k0 += kw
    o_ref[...] = acc


def _proj_residual(lhs_list, w, idx, x, bias=None, *, tm):
    m, n = x.shape
    k = w.shape[1]
    in_specs = [pl.BlockSpec((tm, lhs.shape[1]), lambda i: (i, 0)) for lhs in lhs_list]
    in_specs += [pl.BlockSpec((None, k, n), lambda i: (idx, 0, 0), pipeline_mode=pl.Buffered(1)),
                 pl.BlockSpec((tm, n), lambda i: (i, 0))]
    args = list(lhs_list) + [w, x]
    if bias is not None:
        in_specs.append(pl.BlockSpec((1, n), lambda i: (0, 0)))
        args.append(bias.reshape(1, n))
    return pl.pallas_call(
        functools.partial(_proj_residual_kernel, n_lhs=len(lhs_list), has_bias=bias is not None),
        grid=(m // tm,),
        in_specs=in_specs,
        out_specs=pl.BlockSpec((tm, n), lambda i: (i, 0)),
        out_shape=jax.ShapeDtypeStruct((m, n), F32),
        scratch_shapes=[pltpu.VMEM((k, n), BF16)],
        compiler_params=_params("arbitrary"),
        name="proj_residual",
    )(*args)


def _ffn_kernel(x_ref, g_ref, wg_ref, wu_ref, wd_ref, fg_ref, o_ref, h_ref, *, final_norm):
    f = pl.program_id(1)

    @pl.when(f == 0)
    def _():
        x = x_ref[...]
        h_ref[...] = _rms(x, g_ref[...]).astype(BF16)
        o_ref[...] = x

    h = h_ref[...]
    gate = jnp.dot(h, wg_ref[...].astype(BF16), preferred_element_type=F32)
    up = jnp.dot(h, wu_ref[...].astype(BF16), preferred_element_type=F32)
    act = (_silu(gate) * up).astype(BF16)
    o_ref[...] += jnp.dot(act, wd_ref[...].astype(BF16), preferred_element_type=F32)

    if final_norm:
        @pl.when(f == pl.num_programs(1) - 1)
        def _():
            o_ref[...] = _rms(o_ref[...], fg_ref[...])


def _ffn(x, g, w_gate, w_up, w_down, idx, final_g, *, tm, tf, final_norm):
    m, d = x.shape
    dff = w_gate.shape[2]
    return pl.pallas_call(
        functools.partial(_ffn_kernel, final_norm=final_norm),
        grid=(m // tm, dff // tf),
        in_specs=[
            pl.BlockSpec((tm, d), lambda i, f: (i, 0)),
            pl.BlockSpec((1, d), lambda i, f: (0, 0)),
            pl.BlockSpec((None, d, tf), lambda i, f: (idx, 0, f)),
            pl.BlockSpec((None, d, tf), lambda i, f: (idx, 0, f)),
            pl.BlockSpec((None, tf, d), lambda i, f: (idx, f, 0)),
            pl.BlockSpec((1, d), lambda i, f: (0, 0)),
        ],
        out_specs=pl.BlockSpec((tm, d), lambda i, f: (i, 0)),
        out_shape=jax.ShapeDtypeStruct((m, d), F32),
        scratch_shapes=[pltpu.VMEM((tm, d), BF16)],
        compiler_params=_params("parallel", "arbitrary"),
        name="ffn",
    )(x, g.reshape(1, d), w_gate, w_up, w_down, final_g.reshape(1, d))


def _fnet_channel_kernel(x_ref, g_ref, cs_ref, a_ref, b_ref):
    h = _rms(x_ref[...], g_ref[...]).astype(BF16)
    gd = cs_ref.shape[0]
    for grp in range(FNET_GROUPS):
        cols = slice(grp * gd, (grp + 1) * gd)
        ab = jnp.dot(h[:, cols], cs_ref[...], preferred_element_type=F32)
        a_ref[:, cols] = ab[:, :gd].astype(a_ref.dtype)
        b_ref[:, cols] = ab[:, gd:].astype(b_ref.dtype)


def _fnet_channel(x, g, cs, *, tm):
    m, d = x.shape
    gd = cs.shape[0]
    out = jax.ShapeDtypeStruct((m, d), BF16)
    return pl.pallas_call(
        _fnet_channel_kernel,
        grid=(m // tm,),
        in_specs=[
            pl.BlockSpec((tm, d), lambda i: (i, 0)),
            pl.BlockSpec((1, d), lambda i: (0, 0)),
            pl.BlockSpec((gd, 2 * gd), lambda i: (0, 0)),
        ],
        out_specs=[pl.BlockSpec((tm, d), lambda i: (i, 0))] * 2,
        out_shape=[out, out],
        compiler_params=_params("parallel"),
        name="fnet_channel",
    )(x, g.reshape(1, d), cs)


def _fnet_seq_kernel(t_ref, p_ref, a_ref, b_ref, o_ref, uv_ref, y_ref):
    q = t_ref.shape[1]

    def fold(rows):
        a0, a1, a2, a3 = [a_ref[pl.ds(m * q + rows.start, rows.size), :].astype(F32) for m in range(FFT_RADIX)]
        b0, b1, b2, b3 = [b_ref[pl.ds(m * q + rows.start, rows.size), :].astype(F32) for m in range(FFT_RADIX)]
        ae, ao, ad, aq = a0 + a2, a1 + a3, a0 - a2, a1 - a3
        be, bo, bd, bq = b0 + b2, b1 + b3, b0 - b2, b1 - b3
        u = (ae + ao, ad - bq, ae - ao, ad + bq)
        v = (be + bo, bd + aq, be - bo, bd - aq)
        for r in range(FFT_RADIX):
            uv_ref[r, rows, :] = u[r].astype(BF16)
            uv_ref[r, pl.ds(q + rows.start, rows.size), :] = v[r].astype(BF16)

    _for_row_chunks(q, fold)
    for r in range(FFT_RADIX):
        y_ref[r] = jnp.dot(t_ref[r], uv_ref[r], preferred_element_type=F32).astype(BF16)
    rows = q // FFT_RADIX
    for kt in range(FFT_RADIX):
        blk = jnp.concatenate([y_ref[r, kt * rows:(kt + 1) * rows, :] for r in range(FFT_RADIX)], axis=0)
        o_ref[kt * q:(kt + 1) * q, :] = jnp.dot(p_ref[...], blk, preferred_element_type=F32).astype(o_ref.dtype)


def _fnet_seq(tables, perm, a, b, *, tn):
    bsz, seq, d = a.shape
    q = seq // FFT_RADIX
    return pl.pallas_call(
        _fnet_seq_kernel,
        grid=(bsz, d // tn),
        in_specs=[
            pl.BlockSpec((FFT_RADIX, q, 2 * q), lambda bb, j: (0, 0, 0)),
            pl.BlockSpec((q, q), lambda bb, j: (0, 0)),
            pl.BlockSpec((None, seq, tn), lambda bb, j: (bb, 0, j)),
            pl.BlockSpec((None, seq, tn), lambda bb, j: (bb, 0, j)),
        ],
        out_specs=pl.BlockSpec((None, seq, tn), lambda bb, j: (bb, 0, j)),
        out_shape=jax.ShapeDtypeStruct((bsz, seq, d), BF16),
        scratch_shapes=[pltpu.VMEM((FFT_RADIX, 2 * q, tn), BF16), pltpu.VMEM((FFT_RADIX, q, tn), BF16)],
        compiler_params=_params("parallel", "parallel"),
        name="fnet_seq",
    )(tables, perm, a, b)


def _interleave_matrix(q):
    dst = jnp.arange(q, dtype=jnp.int32)
    src = (dst % FFT_RADIX) * (q // FFT_RADIX) + dst // FFT_RADIX
    return (src[:, None] == jnp.arange(q, dtype=jnp.int32)[None, :]).astype(BF16)


def _cos_sin(rows, cols, n):
    ang = ((rows[:, None] * cols[None, :]) % n).astype(F32) * (2.0 * math.pi / n)
    return jnp.cos(ang), jnp.sin(ang)


def _dft_tables(n, scale):
    q = 1 << (int(math.log2(n)) // 2)
    cols = jnp.arange(n, dtype=jnp.int32)
    ca, sa = _cos_sin(jnp.arange(n // q, dtype=jnp.int32) * q, cols, n)
    cb, sb = _cos_sin(jnp.arange(q, dtype=jnp.int32), cols, n)
    ca, sa = (ca * scale)[:, None, :], (sa * scale)[:, None, :]
    cb, sb = cb[None, :, :], sb[None, :, :]
    return (ca * cb - sa * sb).reshape(n, n), (sa * cb + ca * sb).reshape(n, n)


def _radix_tables(n):
    q = n // FFT_RADIX
    c4, s4 = _dft_tables(q, 1.0 / math.sqrt(n))
    cr, sr = _cos_sin(jnp.arange(FFT_RADIX, dtype=jnp.int32), jnp.arange(q, dtype=jnp.int32), n)
    cr, sr = cr[:, None, :], sr[:, None, :]
    c = c4[None] * cr - s4[None] * sr
    sn = s4[None] * cr + c4[None] * sr
    return jnp.concatenate([c, -sn], axis=2)


def kernel(x, mix_norm_g, ffn_norm_g, final_norm_g, ab_w_in, conv_dw_w, conv_dw_b, conv_ln_g, conv_ln_b,
           sgu_ln_g, sgu_ln_b, sgu_w, sgu_b, ab_w_out, fnet_w_out, fnet_b_out, ffn_w_gate, ffn_w_up, ffn_w_down):
    bsz, seq, d = x.shape
    m = bsz * seq
    depth = mix_norm_g.shape[0]
    xf = x.reshape(m, d)
    for layer in range(depth):
        if layer % 2 == 0:
            i = layer // 2
            y_sgu, h = _sgu_branch(xf, mix_norm_g[layer], ab_w_in, i, sgu_ln_g[i], sgu_ln_b[i], sgu_w[i], sgu_b[i],
                                   tm=512)
            y_conv = _conv_branch(h.reshape(bsz, seq, d), ab_w_in, i, conv_dw_w[i], conv_dw_b[i], conv_ln_g[i],
                                  conv_ln_b[i])
            xf = _proj_residual([y_conv.reshape(m, -1), y_sgu], ab_w_out, i, xf, tm=512)
        else:
            j = layer // 2
            gd = d // FNET_GROUPS
            cc, sc = _dft_tables(gd, 1.0 / math.sqrt(gd))
            a, b = _fnet_channel(xf, mix_norm_g[layer], jnp.concatenate([cc, sc], axis=1).astype(BF16), tm=512)
            y = _fnet_seq(_radix_tables(seq).astype(BF16), _interleave_matrix(seq // FFT_RADIX),
                          a.reshape(bsz, seq, d), b.reshape(bsz, seq, d), tn=1024)
            xf = _proj_residual([y.reshape(m, d)], fnet_w_out, j, xf, fnet_b_out[j], tm=512)
        xf = _ffn(xf, ffn_norm_g[layer], ffn_w_gate, ffn_w_up, ffn_w_down, layer, final_norm_g,
                  tm=1024, tf=256, final_norm=layer == depth - 1)
    return xf.reshape(bsz, seq, d)
```

```python
import functools
import math

import jax
import jax.numpy as jnp
from jax import lax
from jax.experimental import pallas as pl
from jax.experimental.pallas import tpu as pltpu

F32 = jnp.float32
BF16 = jnp.bfloat16

RMS_EPS = 1e-6
LN_EPS = 1e-5
CONV_GROUP_DIM = 128
CONV_WIDTH = 31
CONV_PAD = (CONV_WIDTH - 1) // 2
CONV_HALO = 16
CONV_ROWS = 64
CONV_BLOCK = 256
SUBLANES = 8
SGU_HEADS = 8
CHUNK = 128
FNET_GROUPS = 8
FFT_RADIX = 4

V7X_VMEM_BYTES = 64 * 1024 * 1024
VMEM_LIMIT_BYTES = V7X_VMEM_BYTES - 8 * 1024 * 1024
ROW_CHUNK = 256


def _params(*semantics, flags=None):
    return pltpu.CompilerParams(dimension_semantics=semantics, vmem_limit_bytes=VMEM_LIMIT_BYTES, flags=flags)


def _rms(x, g):
    return x * lax.rsqrt(jnp.mean(x * x, axis=-1, keepdims=True) + RMS_EPS) * g


def _layer_norm(x, g, b):
    mu = jnp.mean(x, axis=-1, keepdims=True)
    xc = x - mu
    return xc * lax.rsqrt(jnp.mean(xc * xc, axis=-1, keepdims=True) + LN_EPS) * g + b


def _gelu(x):
    return 0.5 * x * (1.0 + lax.erf(x * (1.0 / math.sqrt(2.0))))


def _silu(x):
    return x * jax.nn.sigmoid(x)


def _for_row_chunks(nrows, fn):
    def body(r, carry):
        fn(pl.ds(pl.multiple_of(r * ROW_CHUNK, ROW_CHUNK), ROW_CHUNK))
        return carry
    lax.fori_loop(0, nrows // ROW_CHUNK, body, 0)


def _cast_weight(w_ref, wb_ref):
    def cast(rows):
        wb_ref[rows, :] = w_ref[rows, :].astype(BF16)
    _for_row_chunks(w_ref.shape[0], cast)


def _sgu_branch_kernel(x_ref, g_ref, w_ref, lg_ref, lb_ref, ws_ref, bs_ref, o_ref, h_ref, wb_ref, z_a, z_b, *, tm, nt):
    t = pl.program_id(0)
    gdim = o_ref.shape[-1]
    hd = gdim // SGU_HEADS

    def project(z_w):
        for r in range(tm // ROW_CHUNK):
            rows = slice(r * ROW_CHUNK, (r + 1) * ROW_CHUNK)
            h = _rms(x_ref[rows, :], g_ref[...]).astype(BF16)
            h_ref[rows, :] = h
            z_w[rows, :] = jnp.dot(h, wb_ref[...], preferred_element_type=F32)

    def gate(z_r):
        for c in range(tm // CHUNK):
            rows = slice(c * CHUNK, (c + 1) * CHUNK)
            v = _layer_norm(_gelu(z_r[rows, gdim:]), lg_ref[...], lb_ref[...]).astype(BF16)
            for hh in range(SGU_HEADS):
                cols = slice(hh * hd, (hh + 1) * hd)
                mixed = jnp.dot(ws_ref[hh].astype(BF16), v[:, cols], preferred_element_type=F32)
                mixed = mixed + bs_ref[:, hh:hh + 1]
                o_ref[rows, cols] = (_gelu(z_r[rows, cols]) * mixed).astype(o_ref.dtype)

    @pl.when(t == 0)
    def _():
        _cast_weight(w_ref, wb_ref)
        project(z_a)

    for parity, (z_w, z_r) in enumerate(((z_a, z_b), (z_b, z_a))):
        @pl.when((t > 0) & (t < nt) & (t % 2 == parity))
        def _():
            project(z_w)
            gate(z_r)

    @pl.when(t == nt)
    def _():
        gate(z_a if nt % 2 == 1 else z_b)


def _sgu_branch(x, g, w_in, idx, ln_g, ln_b, w_s, b_s, *, tm):
    m, d = x.shape
    gdim = ln_g.shape[0]
    wblk = w_in.shape[2] // (2 * gdim) - 1
    nt = m // tm

    def proj(t):
        return jnp.minimum(t, nt - 1)

    def gated(t):
        return jnp.maximum(t - 1, 0)

    return pl.pallas_call(
        functools.partial(_sgu_branch_kernel, tm=tm, nt=nt),
        grid=(nt + 1,),
        in_specs=[
            pl.BlockSpec((tm, d), lambda t: (proj(t), 0)),
            pl.BlockSpec((1, d), lambda t: (0, 0)),
            pl.BlockSpec((None, d, 2 * gdim), lambda t: (idx, 0, wblk), pipeline_mode=pl.Buffered(1)),
            pl.BlockSpec((1, gdim), lambda t: (0, 0)),
            pl.BlockSpec((1, gdim), lambda t: (0, 0)),
            pl.BlockSpec((SGU_HEADS, CHUNK, CHUNK), lambda t: (0, 0, 0)),
            pl.BlockSpec((CHUNK, SGU_HEADS), lambda t: (0, 0)),
        ],
        out_specs=[pl.BlockSpec((tm, gdim), lambda t: (gated(t), 0)), pl.BlockSpec((tm, d), lambda t: (proj(t), 0))],
        out_shape=[jax.ShapeDtypeStruct((m, gdim), BF16), jax.ShapeDtypeStruct((m, d), BF16)],
        scratch_shapes=[pltpu.VMEM((d, 2 * gdim), BF16), pltpu.VMEM((tm, 2 * gdim), F32),
                        pltpu.VMEM((tm, 2 * gdim), F32)],
        compiler_params=_params("arbitrary"),
        name="sgu_branch",
    )(x, g.reshape(1, d), w_in, ln_g.reshape(1, gdim), ln_b.reshape(1, gdim), w_s, b_s.T)


def _conv_step(h_ref, wa_ref, wg_ref, dwb_ref, b_ref, lg_ref, lb_ref, o_ref, stage_w, stage_r, *, seq):
    cw = o_ref.shape[-1]
    gd = CONV_GROUP_DIM
    w = jnp.concatenate([wa_ref[...].astype(BF16), wg_ref[...].astype(BF16)], axis=1)

    def glu(c):
        z = jnp.dot(h_ref[c * ROW_CHUNK:(c + 1) * ROW_CHUNK, :], w, preferred_element_type=F32)
        start = CONV_HALO + c * ROW_CHUNK
        u = z[:, :cw] * jax.nn.sigmoid(z[:, cw:])
        for grp in range(cw // gd):
            stage_w[grp, start:start + ROW_CHUNK, :] = u[:, grp * gd:(grp + 1) * gd]

    def conv(c):
        for s in range(ROW_CHUNK // CONV_ROWS):
            r0 = c * ROW_CHUNK + s * CONV_ROWS
            base = r0 + CONV_HALO - CONV_PAD
            for grp in range(cw // gd):
                cols = slice(grp * gd, (grp + 1) * gd)
                acc = jnp.zeros((CONV_ROWS // SUBLANES, SUBLANES, gd), F32)
                for k in range(CONV_WIDTH):
                    tap = stage_r[grp, base + k:base + k + CONV_ROWS, :].reshape(CONV_ROWS // SUBLANES, SUBLANES, gd)
                    acc = acc + tap * dwb_ref[k * SUBLANES:(k + 1) * SUBLANES, cols]
                y = _layer_norm(acc.reshape(CONV_ROWS, gd) + b_ref[:, cols], lg_ref[:, cols], lb_ref[:, cols])
                o_ref[r0:r0 + CONV_ROWS, cols] = _silu(y).astype(o_ref.dtype)

    for c in range(seq // ROW_CHUNK):
        glu(c)
        conv(c)


def _conv_branch_kernel(h_ref, wa_ref, wg_ref, dw_ref, b_ref, lg_ref, lb_ref, o_ref, stage_a, stage_b, dwb_ref, *, seq):
    t = pl.program_id(0)
    cw = o_ref.shape[-1]

    @pl.when(t == 0)
    def _():
        stage_a[...] = jnp.zeros_like(stage_a)
        stage_b[...] = jnp.zeros_like(stage_b)

    for k in range(CONV_WIDTH):
        dwb_ref[k * SUBLANES:(k + 1) * SUBLANES, :] = jnp.broadcast_to(dw_ref[k:k + 1, :], (SUBLANES, cw))

    step = functools.partial(_conv_step, h_ref, wa_ref, wg_ref, dwb_ref, b_ref, lg_ref, lb_ref, o_ref, seq=seq)

    @pl.when(t % 2 == 0)
    def _():
        step(stage_a, stage_b)

    @pl.when(t % 2 == 1)
    def _():
        step(stage_b, stage_a)


def _conv_branch(h, w_in, idx, dw_w, dw_b, ln_g, ln_b):
    bsz, seq, d = h.shape
    cdim = dw_w.shape[1]
    cw = CONV_BLOCK
    nblk = cdim // cw
    items = bsz * nblk

    def proj(t):
        return jnp.minimum(t, items - 1)

    def conv(t):
        return jnp.maximum(t - 1, 0)

    vec = pl.BlockSpec((1, cw), lambda t: (0, conv(t) % nblk))
    stage = (cw // CONV_GROUP_DIM, seq + 2 * CONV_HALO, CONV_GROUP_DIM)
    return pl.pallas_call(
        functools.partial(_conv_branch_kernel, seq=seq),
        grid=(items + 1,),
        in_specs=[
            pl.BlockSpec((None, seq, d), lambda t: (proj(t) // nblk, 0, 0)),
            pl.BlockSpec((None, d, cw), lambda t: (idx, 0, proj(t) % nblk)),
            pl.BlockSpec((None, d, cw), lambda t: (idx, 0, nblk + proj(t) % nblk)),
            pl.BlockSpec((CONV_WIDTH, cw), lambda t: (0, conv(t) % nblk)),
            vec, vec, vec,
        ],
        out_specs=pl.BlockSpec((None, seq, cw), lambda t: (conv(t) // nblk, 0, conv(t) % nblk)),
        out_shape=jax.ShapeDtypeStruct((bsz, seq, cdim), BF16),
        scratch_shapes=[pltpu.VMEM(stage, F32), pltpu.VMEM(stage, F32), pltpu.VMEM((CONV_WIDTH * SUBLANES, cw), F32)],
        compiler_params=_params("arbitrary"),
        name="conv_branch",
    )(h, w_in, w_in, dw_w, dw_b.reshape(1, cdim), ln_g.reshape(1, cdim), ln_b.reshape(1, cdim))


def _proj_residual_kernel(*refs, n_lhs, has_bias):
    lhs_refs = refs[:n_lhs]
    w_ref, x_ref = refs[n_lhs], refs[n_lhs + 1]
    o_ref, wb_ref = refs[-2], refs[-1]

    @pl.when(pl.program_id(0) == 0)
    def _():
        _cast_weight(w_ref, wb_ref)

    acc = x_ref[...]
    if has_bias:
        acc = acc + refs[n_lhs + 2][...]
    k0 = 0
    for lhs_ref in lhs_refs:
        kw = lhs_ref.shape[-1]
        acc = acc + jnp.dot(lhs_ref[...], wb_ref[k0:k0 + kw, :], preferred_element_type=F32)
        k0 += kw
    o_ref[...] = acc


def _proj_residual(lhs_list, w, idx, x, bias=None, *, tm):
    m, n = x.shape
    k = w.shape[1]
    in_specs = [pl.BlockSpec((tm, lhs.shape[1]), lambda i: (i, 0)) for lhs in lhs_list]
    in_specs += [pl.BlockSpec((None, k, n), lambda i: (idx, 0, 0), pipeline_mode=pl.Buffered(1)),
                 pl.BlockSpec((tm, n), lambda i: (i, 0))]
    args = list(lhs_list) + [w, x]
    if bias is not None:
        in_specs.append(pl.BlockSpec((1, n), lambda i: (0, 0)))
        args.append(bias.reshape(1, n))
    return pl.pallas_call(
        functools.partial(_proj_residual_kernel, n_lhs=len(lhs_list), has_bias=bias is not None),
        grid=(m // tm,),
        in_specs=in_specs,
        out_specs=pl.BlockSpec((tm, n), lambda i: (i, 0)),
        out_shape=jax.ShapeDtypeStruct((m, n), F32),
        scratch_shapes=[pltpu.VMEM((k, n), BF16)],
        compiler_params=_params("arbitrary"),
        name="proj_residual",
    )(*args)


def _ffn_kernel(x_ref, g_ref, wg_ref, wu_ref, wd_ref, fg_ref, o_ref, h_ref, act_a, act_b, *, nf, final_norm):
    f = pl.program_id(1)

    def project(act_w):
        h = h_ref[...]
        gate = jnp.dot(h, wg_ref[...].astype(BF16), preferred_element_type=F32)
        up = jnp.dot(h, wu_ref[...].astype(BF16), preferred_element_type=F32)
        act_w[...] = (_silu(gate) * up).astype(BF16)

    def down(act_r):
        o_ref[...] += jnp.dot(act_r[...], wd_ref[...].astype(BF16), preferred_element_type=F32)

    @pl.when(f == 0)
    def _():
        x = x_ref[...]
        h_ref[...] = _rms(x, g_ref[...]).astype(BF16)
        o_ref[...] = x
        project(act_a)

    for parity, (act_w, act_r) in enumerate(((act_a, act_b), (act_b, act_a))):
        @pl.when((f > 0) & (f < nf) & (f % 2 == parity))
        def _():
            project(act_w)
            down(act_r)

    @pl.when(f == nf)
    def _():
        down(act_a if nf % 2 == 1 else act_b)
        if final_norm:
            o_ref[...] = _rms(o_ref[...], fg_ref[...])


def _ffn(x, g, w_gate, w_up, w_down, idx, final_g, *, tm, tf, final_norm):
    m, d = x.shape
    dff = w_gate.shape[2]
    nf = dff // tf
    return pl.pallas_call(
        functools.partial(_ffn_kernel, nf=nf, final_norm=final_norm),
        grid=(m // tm, nf + 1),
        in_specs=[
            pl.BlockSpec((tm, d), lambda i, f: (i, 0)),
            pl.BlockSpec((1, d), lambda i, f: (0, 0)),
            pl.BlockSpec((None, d, tf), lambda i, f: (idx, 0, jnp.minimum(f, nf - 1))),
            pl.BlockSpec((None, d, tf), lambda i, f: (idx, 0, jnp.minimum(f, nf - 1))),
            pl.BlockSpec((None, tf, d), lambda i, f: (idx, jnp.maximum(f - 1, 0), 0)),
            pl.BlockSpec((1, d), lambda i, f: (0, 0)),
        ],
        out_specs=pl.BlockSpec((tm, d), lambda i, f: (i, 0)),
        out_shape=jax.ShapeDtypeStruct((m, d), F32),
        scratch_shapes=[pltpu.VMEM((tm, d), BF16), pltpu.VMEM((tm, tf), BF16), pltpu.VMEM((tm, tf), BF16)],
        compiler_params=_params("parallel", "arbitrary"),
        name="ffn",
    )(x, g.reshape(1, d), w_gate, w_up, w_down, final_g.reshape(1, d))


def _fnet_channel_kernel(x_ref, g_ref, cs_ref, a_ref, b_ref):
    h = _rms(x_ref[...], g_ref[...]).astype(BF16)
    gd = cs_ref.shape[0]
    for grp in range(FNET_GROUPS):
        cols = slice(grp * gd, (grp + 1) * gd)
        ab = jnp.dot(h[:, cols], cs_ref[...], preferred_element_type=F32)
        a_ref[:, cols] = ab[:, :gd].astype(a_ref.dtype)
        b_ref[:, cols] = ab[:, gd:].astype(b_ref.dtype)


def _fnet_channel(x, g, cs, *, tm):
    m, d = x.shape
    gd = cs.shape[0]
    out = jax.ShapeDtypeStruct((m, d), BF16)
    return pl.pallas_call(
        _fnet_channel_kernel,
        grid=(m // tm,),
        in_specs=[
            pl.BlockSpec((tm, d), lambda i: (i, 0)),
            pl.BlockSpec((1, d), lambda i: (0, 0)),
            pl.BlockSpec((gd, 2 * gd), lambda i: (0, 0)),
        ],
        out_specs=[pl.BlockSpec((tm, d), lambda i: (i, 0))] * 2,
        out_shape=[out, out],
        compiler_params=_params("parallel"),
        name="fnet_channel",
    )(x, g.reshape(1, d), cs)


def _fnet_seq_kernel(t_ref, p_ref, a_ref, b_ref, o_ref, uv_ref, y_ref):
    q = t_ref.shape[1]

    def fold(rows):
        a0, a1, a2, a3 = [a_ref[pl.ds(m * q + rows.start, rows.size), :].astype(F32) for m in range(FFT_RADIX)]
        b0, b1, b2, b3 = [b_ref[pl.ds(m * q + rows.start, rows.size), :].astype(F32) for m in range(FFT_RADIX)]
        ae, ao, ad, aq = a0 + a2, a1 + a3, a0 - a2, a1 - a3
        be, bo, bd, bq = b0 + b2, b1 + b3, b0 - b2, b1 - b3
        u = (ae + ao, ad - bq, ae - ao, ad + bq)
        v = (be + bo, bd + aq, be - bo, bd - aq)
        for r in range(FFT_RADIX):
            uv_ref[r, rows, :] = u[r].astype(BF16)
            uv_ref[r, pl.ds(q + rows.start, rows.size), :] = v[r].astype(BF16)

    _for_row_chunks(q, fold)
    for r in range(FFT_RADIX):
        y_ref[r] = jnp.dot(t_ref[r], uv_ref[r], preferred_element_type=F32).astype(BF16)
    rows = q // FFT_RADIX
    for kt in range(FFT_RADIX):
        blk = jnp.concatenate([y_ref[r, kt * rows:(kt + 1) * rows, :] for r in range(FFT_RADIX)], axis=0)
        o_ref[kt * q:(kt + 1) * q, :] = jnp.dot(p_ref[...], blk, preferred_element_type=F32).astype(o_ref.dtype)


def _fnet_seq(tables, perm, a, b, *, tn):
    bsz, seq, d = a.shape
    q = seq // FFT_RADIX
    return pl.pallas_call(
        _fnet_seq_kernel,
        grid=(bsz, d // tn),
        in_specs=[
            pl.BlockSpec((FFT_RADIX, q, 2 * q), lambda bb, j: (0, 0, 0)),
            pl.BlockSpec((q, q), lambda bb, j: (0, 0)),
            pl.BlockSpec((None, seq, tn), lambda bb, j: (bb, 0, j)),
            pl.BlockSpec((None, seq, tn), lambda bb, j: (bb, 0, j)),
        ],
        out_specs=pl.BlockSpec((None, seq, tn), lambda bb, j: (bb, 0, j)),
        out_shape=jax.ShapeDtypeStruct((bsz, seq, d), BF16),
        scratch_shapes=[pltpu.VMEM((FFT_RADIX, 2 * q, tn), BF16), pltpu.VMEM((FFT_RADIX, q, tn), BF16)],
        compiler_params=_params("parallel", "parallel"),
        name="fnet_seq",
    )(tables, perm, a, b)


def _interleave_matrix(q):
    dst = jnp.arange(q, dtype=jnp.int32)
    src = (dst % FFT_RADIX) * (q // FFT_RADIX) + dst // FFT_RADIX
    return (src[:, None] == jnp.arange(q, dtype=jnp.int32)[None, :]).astype(BF16)


def _cos_sin(rows, cols, n):
    ang = ((rows[:, None] * cols[None, :]) % n).astype(F32) * (2.0 * math.pi / n)
    return jnp.cos(ang), jnp.sin(ang)


def _dft_tables(n, scale):
    q = 1 << (int(math.log2(n)) // 2)
    cols = jnp.arange(n, dtype=jnp.int32)
    ca, sa = _cos_sin(jnp.arange(n // q, dtype=jnp.int32) * q, cols, n)
    cb, sb = _cos_sin(jnp.arange(q, dtype=jnp.int32), cols, n)
    ca, sa = (ca * scale)[:, None, :], (sa * scale)[:, None, :]
    cb, sb = cb[None, :, :], sb[None, :, :]
    return (ca * cb - sa * sb).reshape(n, n), (sa * cb + ca * sb).reshape(n, n)


def _radix_tables(n):
    q = n // FFT_RADIX
    c4, s4 = _dft_tables(q, 1.0 / math.sqrt(n))
    cr, sr = _cos_sin(jnp.arange(FFT_RADIX, dtype=jnp.int32), jnp.arange(q, dtype=jnp.int32), n)
    cr, sr = cr[:, None, :], sr[:, None, :]
    c = c4[None] * cr - s4[None] * sr
    sn = s4[None] * cr + c4[None] * sr
    return jnp.concatenate([c, -sn], axis=2)


def kernel(x, mix_norm_g, ffn_norm_g, final_norm_g, ab_w_in, conv_dw_w, conv_dw_b, conv_ln_g, conv_ln_b,
           sgu_ln_g, sgu_ln_b, sgu_w, sgu_b, ab_w_out, fnet_w_out, fnet_b_out, ffn_w_gate, ffn_w_up, ffn_w_down):
    bsz, seq, d = x.shape
    m = bsz * seq
    depth = mix_norm_g.shape[0]
    xf = x.reshape(m, d)
    for layer in range(depth):
        if layer % 2 == 0:
            i = layer // 2
            y_sgu, h = _sgu_branch(xf, mix_norm_g[layer], ab_w_in, i, sgu_ln_g[i], sgu_ln_b[i], sgu_w[i], sgu_b[i],
                                   tm=512)
            y_conv = _conv_branch(h.reshape(bsz, seq, d), ab_w_in, i, conv_dw_w[i], conv_dw_b[i], conv_ln_g[i],
                                  conv_ln_b[i])
            xf = _proj_residual([y_conv.reshape(m, -1), y_sgu], ab_w_out, i, xf, tm=512)
        else:
            j = layer // 2
            gd = d // FNET_GROUPS
            cc, sc = _dft_tables(gd, 1.0 / math.sqrt(gd))
            a, b = _fnet_channel(xf, mix_norm_g[layer], jnp.concatenate([cc, sc], axis=1).astype(BF16), tm=512)
            y = _fnet_seq(_radix_tables(seq).astype(BF16), _interleave_matrix(seq // FFT_RADIX),
                          a.reshape(bsz, seq, d), b.reshape(bsz, seq, d), tn=1024)
            xf = _proj_residual([y.reshape(m, d)], fnet_w_out, j, xf, fnet_b_out[j], tm=512)
        xf = _ffn(xf, ffn_norm_g[layer], ffn_w_gate, ffn_w_up, ffn_w_down, layer, final_norm_g,
                  tm=1024, tf=256, final_norm=layer == depth - 1)
    return xf.reshape(bsz, seq, d)
```

```python
import functools
import math

import jax
import jax.numpy as jnp
from jax import lax
from jax.experimental import pallas as pl
from jax.experimental.pallas import tpu as pltpu

F32 = jnp.float32
BF16 = jnp.bfloat16

RMS_EPS = 1e-6
LN_EPS = 1e-5
CONV_GROUP_DIM = 128
CONV_WIDTH = 31
CONV_PAD = (CONV_WIDTH - 1) // 2
CONV_HALO = 16
CONV_ROWS = 64
CONV_BLOCK = 256
SUBLANES = 8
SGU_HEADS = 8
CHUNK = 128
FNET_GROUPS = 8
FFT_RADIX = 4

V7X_VMEM_BYTES = 64 * 1024 * 1024
VMEM_LIMIT_BYTES = V7X_VMEM_BYTES - 8 * 1024 * 1024
ROW_CHUNK = 256


def _params(*semantics, flags=None):
    return pltpu.CompilerParams(dimension_semantics=semantics, vmem_limit_bytes=VMEM_LIMIT_BYTES, flags=flags)


def _rms(x, g):
    return x * lax.rsqrt(jnp.mean(x * x, axis=-1, keepdims=True) + RMS_EPS) * g


def _layer_norm(x, g, b):
    mu = jnp.mean(x, axis=-1, keepdims=True)
    xc = x - mu
    return xc * lax.rsqrt(jnp.mean(xc * xc, axis=-1, keepdims=True) + LN_EPS) * g + b


def _gelu(x):
    return 0.5 * x * (1.0 + lax.erf(x * (1.0 / math.sqrt(2.0))))


def _silu(x):
    return x * jax.nn.sigmoid(x)


def _for_row_chunks(nrows, fn):
    def body(r, carry):
        fn(pl.ds(pl.multiple_of(r * ROW_CHUNK, ROW_CHUNK), ROW_CHUNK))
        return carry
    lax.fori_loop(0, nrows // ROW_CHUNK, body, 0)


def _cast_weight(w_ref, wb_ref):
    def cast(rows):
        wb_ref[rows, :] = w_ref[rows, :].astype(BF16)
    _for_row_chunks(w_ref.shape[0], cast)


def _sgu_branch_kernel(x_ref, g_ref, w_ref, lg_ref, lb_ref, ws_ref, bs_ref, o_ref, h_ref, wb_ref, z_a, z_b, *, tm, nt):
    t = pl.program_id(0)
    gdim = o_ref.shape[-1]
    hd = gdim // SGU_HEADS

    def project(z_w):
        for r in range(tm // ROW_CHUNK):
            rows = slice(r * ROW_CHUNK, (r + 1) * ROW_CHUNK)
            h = _rms(x_ref[rows, :], g_ref[...]).astype(BF16)
            h_ref[rows, :] = h
            z_w[rows, :] = jnp.dot(h, wb_ref[...], preferred_element_type=F32)

    def gate(z_r):
        for c in range(tm // CHUNK):
            rows = slice(c * CHUNK, (c + 1) * CHUNK)
            v = _layer_norm(_gelu(z_r[rows, gdim:]), lg_ref[...], lb_ref[...]).astype(BF16)
            for hh in range(SGU_HEADS):
                cols = slice(hh * hd, (hh + 1) * hd)
                mixed = jnp.dot(ws_ref[hh].astype(BF16), v[:, cols], preferred_element_type=F32)
                mixed = mixed + bs_ref[:, hh:hh + 1]
                o_ref[rows, cols] = (_gelu(z_r[rows, cols]) * mixed).astype(o_ref.dtype)

    @pl.when(t == 0)
    def _():
        _cast_weight(w_ref, wb_ref)
        project(z_a)

    for parity, (z_w, z_r) in enumerate(((z_a, z_b), (z_b, z_a))):
        @pl.when((t > 0) & (t < nt) & (t % 2 == parity))
        def _():
            project(z_w)
            gate(z_r)

    @pl.when(t == nt)
    def _():
        gate(z_a if nt % 2 == 1 else z_b)


def _sgu_branch(x, g, w_in, idx, ln_g, ln_b, w_s, b_s, *, tm):
    m, d = x.shape
    gdim = ln_g.shape[0]
    wblk = w_in.shape[2] // (2 * gdim) - 1
    nt = m // tm

    def proj(t):
        return jnp.minimum(t, nt - 1)

    def gated(t):
        return jnp.maximum(t - 1, 0)

    return pl.pallas_call(
        functools.partial(_sgu_branch_kernel, tm=tm, nt=nt),
        grid=(nt + 1,),
        in_specs=[
            pl.BlockSpec((tm, d), lambda t: (proj(t), 0)),
            pl.BlockSpec((1, d), lambda t: (0, 0)),
            pl.BlockSpec((None, d, 2 * gdim), lambda t: (idx, 0, wblk), pipeline_mode=pl.Buffered(1)),
            pl.BlockSpec((1, gdim), lambda t: (0, 0)),
            pl.BlockSpec((1, gdim), lambda t: (0, 0)),
            pl.BlockSpec((SGU_HEADS, CHUNK, CHUNK), lambda t: (0, 0, 0)),
            pl.BlockSpec((CHUNK, SGU_HEADS), lambda t: (0, 0)),
        ],
        out_specs=[pl.BlockSpec((tm, gdim), lambda t: (gated(t), 0)), pl.BlockSpec((tm, d), lambda t: (proj(t), 0))],
        out_shape=[jax.ShapeDtypeStruct((m, gdim), BF16), jax.ShapeDtypeStruct((m, d), BF16)],
        scratch_shapes=[pltpu.VMEM((d, 2 * gdim), BF16), pltpu.VMEM((tm, 2 * gdim), F32),
                        pltpu.VMEM((tm, 2 * gdim), F32)],
        compiler_params=_params("arbitrary"),
        name="sgu_branch",
    )(x, g.reshape(1, d), w_in, ln_g.reshape(1, gdim), ln_b.reshape(1, gdim), w_s, b_s.T)


def _conv_step(h_ref, wa_ref, wg_ref, dwb_ref, b_ref, lg_ref, lb_ref, o_ref, stage_w, stage_r, *, seq):
    cw = o_ref.shape[-1]
    gd = CONV_GROUP_DIM
    w = jnp.concatenate([wa_ref[...].astype(BF16), wg_ref[...].astype(BF16)], axis=1)

    def glu(c):
        z = jnp.dot(h_ref[c * ROW_CHUNK:(c + 1) * ROW_CHUNK, :], w, preferred_element_type=F32)
        start = CONV_HALO + c * ROW_CHUNK
        u = z[:, :cw] * jax.nn.sigmoid(z[:, cw:])
        for grp in range(cw // gd):
            stage_w[grp, start:start + ROW_CHUNK, :] = u[:, grp * gd:(grp + 1) * gd]

    def conv(c):
        for s in range(ROW_CHUNK // CONV_ROWS):
            r0 = c * ROW_CHUNK + s * CONV_ROWS
            base = r0 + CONV_HALO - CONV_PAD
            for grp in range(cw // gd):
                cols = slice(grp * gd, (grp + 1) * gd)
                acc = jnp.zeros((CONV_ROWS // SUBLANES, SUBLANES, gd), F32)
                for k in range(CONV_WIDTH):
                    tap = stage_r[grp, base + k:base + k + CONV_ROWS, :].reshape(CONV_ROWS // SUBLANES, SUBLANES, gd)
                    acc = acc + tap * dwb_ref[k * SUBLANES:(k + 1) * SUBLANES, cols]
                y = _layer_norm(acc.reshape(CONV_ROWS, gd) + b_ref[:, cols], lg_ref[:, cols], lb_ref[:, cols])
                o_ref[r0:r0 + CONV_ROWS, cols] = _silu(y).astype(o_ref.dtype)

    for c in range(seq // ROW_CHUNK):
        glu(c)
        conv(c)


def _conv_branch_kernel(h_ref, wa_ref, wg_ref, dw_ref, b_ref, lg_ref, lb_ref, o_ref, stage_a, stage_b, dwb_ref, *, seq):
    t = pl.program_id(0)
    cw = o_ref.shape[-1]

    @pl.when(t == 0)
    def _():
        stage_a[...] = jnp.zeros_like(stage_a)
        stage_b[...] = jnp.zeros_like(stage_b)

    for k in range(CONV_WIDTH):
        dwb_ref[k * SUBLANES:(k + 1) * SUBLANES, :] = jnp.broadcast_to(dw_ref[k:k + 1, :], (SUBLANES, cw))

    step = functools.partial(_conv_step, h_ref, wa_ref, wg_ref, dwb_ref, b_ref, lg_ref, lb_ref, o_ref, seq=seq)

    @pl.when(t % 2 == 0)
    def _():
        step(stage_a, stage_b)

    @pl.when(t % 2 == 1)
    def _():
        step(stage_b, stage_a)


def _conv_branch(h, w_in, idx, dw_w, dw_b, ln_g, ln_b):
    bsz, seq, d = h.shape
    cdim = dw_w.shape[1]
    cw = CONV_BLOCK
    nblk = cdim // cw
    items = bsz * nblk

    def proj(t):
        return jnp.minimum(t, items - 1)

    def conv(t):
        return jnp.maximum(t - 1, 0)

    vec = pl.BlockSpec((1, cw), lambda t: (0, conv(t) % nblk))
    stage = (cw // CONV_GROUP_DIM, seq + 2 * CONV_HALO, CONV_GROUP_DIM)
    return pl.pallas_call(
        functools.partial(_conv_branch_kernel, seq=seq),
        grid=(items + 1,),
        in_specs=[
            pl.BlockSpec((None, seq, d), lambda t: (proj(t) // nblk, 0, 0)),
            pl.BlockSpec((None, d, cw), lambda t: (idx, 0, proj(t) % nblk)),
            pl.BlockSpec((None, d, cw), lambda t: (idx, 0, nblk + proj(t) % nblk)),
            pl.BlockSpec((CONV_WIDTH, cw), lambda t: (0, conv(t) % nblk)),
            vec, vec, vec,
        ],
        out_specs=pl.BlockSpec((None, seq, cw), lambda t: (conv(t) // nblk, 0, conv(t) % nblk)),
        out_shape=jax.ShapeDtypeStruct((bsz, seq, cdim), BF16),
        scratch_shapes=[pltpu.VMEM(stage, F32), pltpu.VMEM(stage, F32), pltpu.VMEM((CONV_WIDTH * SUBLANES, cw), F32)],
        compiler_params=_params("arbitrary"),
        name="conv_branch",
    )(h, w_in, w_in, dw_w, dw_b.reshape(1, cdim), ln_g.reshape(1, cdim), ln_b.reshape(1, cdim))


def _proj_residual_kernel(*refs, n_lhs, has_bias):
    lhs_refs = refs[:n_lhs]
    w_ref, x_ref, g_ref = refs[n_lhs], refs[n_lhs + 1], refs[n_lhs + 2]
    o_ref, h_ref, wb_ref = refs[-3], refs[-2], refs[-1]

    @pl.when(pl.program_id(0) == 0)
    def _():
        _cast_weight(w_ref, wb_ref)

    acc = x_ref[...]
    if has_bias:
        acc = acc + refs[n_lhs + 3][...]
    k0 = 0
    for lhs_ref in lhs_refs:
        kw = lhs_ref.shape[-1]
        acc = acc + jnp.dot(lhs_ref[...], wb_ref[k0:k0 + kw, :], preferred_element_type=F32)
        k0 += kw
    o_ref[...] = acc
    h_ref[...] = _rms(acc, g_ref[...]).astype(BF16)


def _proj_residual(lhs_list, w, idx, x, g, bias=None, *, tm):
    m, n = x.shape
    k = w.shape[1]
    row = pl.BlockSpec((tm, n), lambda i: (i, 0))
    vec = pl.BlockSpec((1, n), lambda i: (0, 0))
    in_specs = [pl.BlockSpec((tm, lhs.shape[1]), lambda i: (i, 0)) for lhs in lhs_list]
    in_specs += [pl.BlockSpec((None, k, n), lambda i: (idx, 0, 0), pipeline_mode=pl.Buffered(1)), row, vec]
    args = list(lhs_list) + [w, x, g.reshape(1, n)]
    if bias is not None:
        in_specs.append(vec)
        args.append(bias.reshape(1, n))
    return pl.pallas_call(
        functools.partial(_proj_residual_kernel, n_lhs=len(lhs_list), has_bias=bias is not None),
        grid=(m // tm,),
        in_specs=in_specs,
        out_specs=[row, row],
        out_shape=[jax.ShapeDtypeStruct((m, n), F32), jax.ShapeDtypeStruct((m, n), BF16)],
        scratch_shapes=[pltpu.VMEM((k, n), BF16)],
        compiler_params=_params("arbitrary"),
        name="proj_residual",
    )(*args)


def _ffn_kernel(x_ref, h_ref, wg_ref, wu_ref, wd_ref, fg_ref, o_ref, *, final_norm):
    f = pl.program_id(1)

    def delta():
        h = h_ref[...]
        gate = jnp.dot(h, wg_ref[...].astype(BF16), preferred_element_type=F32)
        up = jnp.dot(h, wu_ref[...].astype(BF16), preferred_element_type=F32)
        act = (_silu(gate) * up).astype(BF16)
        return jnp.dot(act, wd_ref[...].astype(BF16), preferred_element_type=F32)

    @pl.when(f == 0)
    def _():
        o_ref[...] = x_ref[...] + delta()

    @pl.when(f > 0)
    def _():
        o_ref[...] += delta()

    if final_norm:
        @pl.when(f == pl.num_programs(1) - 1)
        def _():
            o_ref[...] = _rms(o_ref[...], fg_ref[...])


def _ffn(x, h, w_gate, w_up, w_down, idx, final_g, *, tm, tf, final_norm):
    m, d = x.shape
    dff = w_gate.shape[2]
    return pl.pallas_call(
        functools.partial(_ffn_kernel, final_norm=final_norm),
        grid=(m // tm, dff // tf),
        in_specs=[
            pl.BlockSpec((tm, d), lambda i, f: (i, 0)),
            pl.BlockSpec((tm, d), lambda i, f: (i, 0)),
            pl.BlockSpec((None, d, tf), lambda i, f: (idx, 0, f)),
            pl.BlockSpec((None, d, tf), lambda i, f: (idx, 0, f)),
            pl.BlockSpec((None, tf, d), lambda i, f: (idx, f, 0)),
            pl.BlockSpec((1, d), lambda i, f: (0, 0)),
        ],
        out_specs=pl.BlockSpec((tm, d), lambda i, f: (i, 0)),
        out_shape=jax.ShapeDtypeStruct((m, d), F32),
        compiler_params=_params("parallel", "arbitrary"),
        name="ffn",
    )(x, h, w_gate, w_up, w_down, final_g.reshape(1, d))


def _fnet_channel_kernel(x_ref, g_ref, cs_ref, a_ref, b_ref):
    h = _rms(x_ref[...], g_ref[...]).astype(BF16)
    gd = cs_ref.shape[0]
    for grp in range(FNET_GROUPS):
        cols = slice(grp * gd, (grp + 1) * gd)
        ab = jnp.dot(h[:, cols], cs_ref[...], preferred_element_type=F32)
        a_ref[:, cols] = ab[:, :gd].astype(a_ref.dtype)
        b_ref[:, cols] = ab[:, gd:].astype(b_ref.dtype)


def _fnet_channel(x, g, cs, *, tm):
    m, d = x.shape
    gd = cs.shape[0]
    out = jax.ShapeDtypeStruct((m, d), BF16)
    return pl.pallas_call(
        _fnet_channel_kernel,
        grid=(m // tm,),
        in_specs=[
            pl.BlockSpec((tm, d), lambda i: (i, 0)),
            pl.BlockSpec((1, d), lambda i: (0, 0)),
            pl.BlockSpec((gd, 2 * gd), lambda i: (0, 0)),
        ],
        out_specs=[pl.BlockSpec((tm, d), lambda i: (i, 0))] * 2,
        out_shape=[out, out],
        compiler_params=_params("parallel"),
        name="fnet_channel",
    )(x, g.reshape(1, d), cs)


def _fnet_seq_kernel(t_ref, p_ref, a_ref, b_ref, o_ref, uv_ref, y_ref):
    q = t_ref.shape[1]

    def fold(rows):
        a0, a1, a2, a3 = [a_ref[pl.ds(m * q + rows.start, rows.size), :].astype(F32) for m in range(FFT_RADIX)]
        b0, b1, b2, b3 = [b_ref[pl.ds(m * q + rows.start, rows.size), :].astype(F32) for m in range(FFT_RADIX)]
        ae, ao, ad, aq = a0 + a2, a1 + a3, a0 - a2, a1 - a3
        be, bo, bd, bq = b0 + b2, b1 + b3, b0 - b2, b1 - b3
        u = (ae + ao, ad - bq, ae - ao, ad + bq)
        v = (be + bo, bd + aq, be - bo, bd - aq)
        for r in range(FFT_RADIX):
            uv_ref[r, rows, :] = u[r].astype(BF16)
            uv_ref[r, pl.ds(q + rows.start, rows.size), :] = v[r].astype(BF16)

    _for_row_chunks(q, fold)
    for r in range(FFT_RADIX):
        y_ref[r] = jnp.dot(t_ref[r], uv_ref[r], preferred_element_type=F32).astype(BF16)
    rows = q // FFT_RADIX
    for kt in range(FFT_RADIX):
        blk = jnp.concatenate([y_ref[r, kt * rows:(kt + 1) * rows, :] for r in range(FFT_RADIX)], axis=0)
        o_ref[kt * q:(kt + 1) * q, :] = jnp.dot(p_ref[...], blk, preferred_element_type=F32).astype(o_ref.dtype)


def _fnet_seq(tables, perm, a, b, *, tn):
    bsz, seq, d = a.shape
    q = seq // FFT_RADIX
    return pl.pallas_call(
        _fnet_seq_kernel,
        grid=(bsz, d // tn),
        in_specs=[
            pl.BlockSpec((FFT_RADIX, q, 2 * q), lambda bb, j: (0, 0, 0)),
            pl.BlockSpec((q, q), lambda bb, j: (0, 0)),
            pl.BlockSpec((None, seq, tn), lambda bb, j: (bb, 0, j)),
            pl.BlockSpec((None, seq, tn), lambda bb, j: (bb, 0, j)),
        ],
        out_specs=pl.BlockSpec((None, seq, tn), lambda bb, j: (bb, 0, j)),
        out_shape=jax.ShapeDtypeStruct((bsz, seq, d), BF16),
        scratch_shapes=[pltpu.VMEM((FFT_RADIX, 2 * q, tn), BF16), pltpu.VMEM((FFT_RADIX, q, tn), BF16)],
        compiler_params=_params("parallel", "parallel"),
        name="fnet_seq",
    )(tables, perm, a, b)


def _interleave_matrix(q):
    dst = jnp.arange(q, dtype=jnp.int32)
    src = (dst % FFT_RADIX) * (q // FFT_RADIX) + dst // FFT_RADIX
    return (src[:, None] == jnp.arange(q, dtype=jnp.int32)[None, :]).astype(BF16)


def _cos_sin(rows, cols, n):
    ang = ((rows[:, None] * cols[None, :]) % n).astype(F32) * (2.0 * math.pi / n)
    return jnp.cos(ang), jnp.sin(ang)


def _dft_tables(n, scale):
    q = 1 << (int(math.log2(n)) // 2)
    cols = jnp.arange(n, dtype=jnp.int32)
    ca, sa = _cos_sin(jnp.arange(n // q, dtype=jnp.int32) * q, cols, n)
    cb, sb = _cos_sin(jnp.arange(q, dtype=jnp.int32), cols, n)
    ca, sa = (ca * scale)[:, None, :], (sa * scale)[:, None, :]
    cb, sb = cb[None, :, :], sb[None, :, :]
    return (ca * cb - sa * sb).reshape(n, n), (sa * cb + ca * sb).reshape(n, n)


def _radix_tables(n):
    q = n // FFT_RADIX
    c4, s4 = _dft_tables(q, 1.0 / math.sqrt(n))
    cr, sr = _cos_sin(jnp.arange(FFT_RADIX, dtype=jnp.int32), jnp.arange(q, dtype=jnp.int32), n)
    cr, sr = cr[:, None, :], sr[:, None, :]
    c = c4[None] * cr - s4[None] * sr
    sn = s4[None] * cr + c4[None] * sr
    return jnp.concatenate([c, -sn], axis=2)


def kernel(x, mix_norm_g, ffn_norm_g, final_norm_g, ab_w_in, conv_dw_w, conv_dw_b, conv_ln_g, conv_ln_b,
           sgu_ln_g, sgu_ln_b, sgu_w, sgu_b, ab_w_out, fnet_w_out, fnet_b_out, ffn_w_gate, ffn_w_up, ffn_w_down):
    bsz, seq, d = x.shape
    m = bsz * seq
    depth = mix_norm_g.shape[0]
    xf = x.reshape(m, d)
    for layer in range(depth):
        if layer % 2 == 0:
            i = layer // 2
            y_sgu, h = _sgu_branch(xf, mix_norm_g[layer], ab_w_in, i, sgu_ln_g[i], sgu_ln_b[i], sgu_w[i], sgu_b[i],
                                   tm=512)
            y_conv = _conv_branch(h.reshape(bsz, seq, d), ab_w_in, i, conv_dw_w[i], conv_dw_b[i], conv_ln_g[i],
                                  conv_ln_b[i])
            xf, hf = _proj_residual([y_conv.reshape(m, -1), y_sgu], ab_w_out, i, xf, ffn_norm_g[layer], tm=512)
        else:
            j = layer // 2
            gd = d // FNET_GROUPS
            cc, sc = _dft_tables(gd, 1.0 / math.sqrt(gd))
            a, b = _fnet_channel(xf, mix_norm_g[layer], jnp.concatenate([cc, sc], axis=1).astype(BF16), tm=512)
            y = _fnet_seq(_radix_tables(seq).astype(BF16), _interleave_matrix(seq // FFT_RADIX),
                          a.reshape(bsz, seq, d), b.reshape(bsz, seq, d), tn=1024)
            xf, hf = _proj_residual([y.reshape(m, d)], fnet_w_out, j, xf, ffn_norm_g[layer], fnet_b_out[j], tm=512)
        xf = _ffn(xf, hf, ffn_w_gate, ffn_w_up, ffn_w_down, layer, final_norm_g,
                  tm=1024, tf=256, final_norm=layer == depth - 1)
    return xf.reshape(bsz, seq, d)
```

```python
import functools
import math

import jax
import jax.numpy as jnp
from jax import lax
from jax.experimental import pallas as pl
from jax.experimental.pallas import tpu as pltpu

F32 = jnp.float32
BF16 = jnp.bfloat16

RMS_EPS = 1e-6
LN_EPS = 1e-5
CONV_GROUP_DIM = 128
CONV_WIDTH = 31
CONV_PAD = (CONV_WIDTH - 1) // 2
CONV_HALO = 16
CONV_ROWS = 64
CONV_BLOCK = 256
SUBLANES = 8
SGU_HEADS = 8
CHUNK = 128
FNET_GROUPS = 8
FFT_RADIX = 4

V7X_VMEM_BYTES = 64 * 1024 * 1024
VMEM_LIMIT_BYTES = V7X_VMEM_BYTES - 8 * 1024 * 1024
ROW_CHUNK = 256


def _params(*semantics, flags=None):
    return pltpu.CompilerParams(dimension_semantics=semantics, vmem_limit_bytes=VMEM_LIMIT_BYTES, flags=flags)


def _rms(x, g):
    return x * lax.rsqrt(jnp.mean(x * x, axis=-1, keepdims=True) + RMS_EPS) * g


def _layer_norm(x, g, b):
    mu = jnp.mean(x, axis=-1, keepdims=True)
    xc = x - mu
    return xc * lax.rsqrt(jnp.mean(xc * xc, axis=-1, keepdims=True) + LN_EPS) * g + b


def _gelu(x):
    return 0.5 * x * (1.0 + lax.erf(x * (1.0 / math.sqrt(2.0))))


def _silu(x):
    return x * jax.nn.sigmoid(x)


def _for_row_chunks(nrows, fn):
    def body(r, carry):
        fn(pl.ds(pl.multiple_of(r * ROW_CHUNK, ROW_CHUNK), ROW_CHUNK))
        return carry
    lax.fori_loop(0, nrows // ROW_CHUNK, body, 0)


def _cast_weight(w_ref, wb_ref):
    def cast(rows):
        wb_ref[rows, :] = w_ref[rows, :].astype(BF16)
    _for_row_chunks(w_ref.shape[0], cast)


def _sgu_branch_kernel(x_ref, g_ref, w_ref, lg_ref, lb_ref, ws_ref, bs_ref, o_ref, h_ref, wb_ref, z_a, z_b, *, tm, nt):
    t = pl.program_id(0)
    gdim = o_ref.shape[-1]
    hd = gdim // SGU_HEADS

    def project(z_w):
        for r in range(tm // ROW_CHUNK):
            rows = slice(r * ROW_CHUNK, (r + 1) * ROW_CHUNK)
            h = _rms(x_ref[rows, :], g_ref[...]).astype(BF16)
            h_ref[rows, :] = h
            z_w[rows, :] = jnp.dot(h, wb_ref[...], preferred_element_type=F32)

    def gate(z_r):
        for c in range(tm // CHUNK):
            rows = slice(c * CHUNK, (c + 1) * CHUNK)
            v = _layer_norm(_gelu(z_r[rows, gdim:]), lg_ref[...], lb_ref[...]).astype(BF16)
            for hh in range(SGU_HEADS):
                cols = slice(hh * hd, (hh + 1) * hd)
                mixed = jnp.dot(ws_ref[hh].astype(BF16), v[:, cols], preferred_element_type=F32)
                mixed = mixed + bs_ref[:, hh:hh + 1]
                o_ref[rows, cols] = (_gelu(z_r[rows, cols]) * mixed).astype(o_ref.dtype)

    @pl.when(t == 0)
    def _():
        _cast_weight(w_ref, wb_ref)
        project(z_a)

    for parity, (z_w, z_r) in enumerate(((z_a, z_b), (z_b, z_a))):
        @pl.when((t > 0) & (t < nt) & (t % 2 == parity))
        def _():
            project(z_w)
            gate(z_r)

    @pl.when(t == nt)
    def _():
        gate(z_a if nt % 2 == 1 else z_b)


def _sgu_branch(x, g, w_in, idx, ln_g, ln_b, w_s, b_s, *, tm):
    m, d = x.shape
    gdim = ln_g.shape[0]
    wblk = w_in.shape[2] // (2 * gdim) - 1
    nt = m // tm

    def proj(t):
        return jnp.minimum(t, nt - 1)

    def gated(t):
        return jnp.maximum(t - 1, 0)

    return pl.pallas_call(
        functools.partial(_sgu_branch_kernel, tm=tm, nt=nt),
        grid=(nt + 1,),
        in_specs=[
            pl.BlockSpec((tm, d), lambda t: (proj(t), 0)),
            pl.BlockSpec((1, d), lambda t: (0, 0)),
            pl.BlockSpec((None, d, 2 * gdim), lambda t: (idx, 0, wblk), pipeline_mode=pl.Buffered(1)),
            pl.BlockSpec((1, gdim), lambda t: (0, 0)),
            pl.BlockSpec((1, gdim), lambda t: (0, 0)),
            pl.BlockSpec((SGU_HEADS, CHUNK, CHUNK), lambda t: (0, 0, 0)),
            pl.BlockSpec((CHUNK, SGU_HEADS), lambda t: (0, 0)),
        ],
        out_specs=[pl.BlockSpec((tm, gdim), lambda t: (gated(t), 0)), pl.BlockSpec((tm, d), lambda t: (proj(t), 0))],
        out_shape=[jax.ShapeDtypeStruct((m, gdim), BF16), jax.ShapeDtypeStruct((m, d), BF16)],
        scratch_shapes=[pltpu.VMEM((d, 2 * gdim), BF16), pltpu.VMEM((tm, 2 * gdim), F32),
                        pltpu.VMEM((tm, 2 * gdim), F32)],
        compiler_params=_params("arbitrary"),
        name="sgu_branch",
    )(x, g.reshape(1, d), w_in, ln_g.reshape(1, gdim), ln_b.reshape(1, gdim), w_s, b_s.T)


def _conv_step(h_ref, wa_ref, wg_ref, dwb_ref, b_ref, lg_ref, lb_ref, o_ref, stage_w, stage_r, *, seq):
    cw = o_ref.shape[-1]
    gd = CONV_GROUP_DIM
    w = jnp.concatenate([wa_ref[...].astype(BF16), wg_ref[...].astype(BF16)], axis=1)

    def glu(c):
        z = jnp.dot(h_ref[c * ROW_CHUNK:(c + 1) * ROW_CHUNK, :], w, preferred_element_type=F32)
        start = CONV_HALO + c * ROW_CHUNK
        u = z[:, :cw] * jax.nn.sigmoid(z[:, cw:])
        for grp in range(cw // gd):
            stage_w[grp, start:start + ROW_CHUNK, :] = u[:, grp * gd:(grp + 1) * gd]

    def conv(c):
        for s in range(ROW_CHUNK // CONV_ROWS):
            r0 = c * ROW_CHUNK + s * CONV_ROWS
            base = r0 + CONV_HALO - CONV_PAD
            for grp in range(cw // gd):
                cols = slice(grp * gd, (grp + 1) * gd)
                acc = jnp.zeros((CONV_ROWS // SUBLANES, SUBLANES, gd), F32)
                for k in range(CONV_WIDTH):
                    tap = stage_r[grp, base + k:base + k + CONV_ROWS, :].reshape(CONV_ROWS // SUBLANES, SUBLANES, gd)
                    acc = acc + tap * dwb_ref[k * SUBLANES:(k + 1) * SUBLANES, cols]
                y = _layer_norm(acc.reshape(CONV_ROWS, gd) + b_ref[:, cols], lg_ref[:, cols], lb_ref[:, cols])
                o_ref[r0:r0 + CONV_ROWS, cols] = _silu(y).astype(o_ref.dtype)

    for c in range(seq // ROW_CHUNK):
        glu(c)
        conv(c)


def _conv_branch_kernel(h_ref, wa_ref, wg_ref, dw_ref, b_ref, lg_ref, lb_ref, o_ref, stage_a, stage_b, dwb_ref, *, seq):
    t = pl.program_id(0)
    cw = o_ref.shape[-1]

    @pl.when(t == 0)
    def _():
        stage_a[...] = jnp.zeros_like(stage_a)
        stage_b[...] = jnp.zeros_like(stage_b)

    for k in range(CONV_WIDTH):
        dwb_ref[k * SUBLANES:(k + 1) * SUBLANES, :] = jnp.broadcast_to(dw_ref[k:k + 1, :], (SUBLANES, cw))

    step = functools.partial(_conv_step, h_ref, wa_ref, wg_ref, dwb_ref, b_ref, lg_ref, lb_ref, o_ref, seq=seq)

    @pl.when(t % 2 == 0)
    def _():
        step(stage_a, stage_b)

    @pl.when(t % 2 == 1)
    def _():
        step(stage_b, stage_a)


def _conv_branch(h, w_in, idx, dw_w, dw_b, ln_g, ln_b):
    bsz, seq, d = h.shape
    cdim = dw_w.shape[1]
    cw = CONV_BLOCK
    nblk = cdim // cw
    items = bsz * nblk

    def proj(t):
        return jnp.minimum(t, items - 1)

    def conv(t):
        return jnp.maximum(t - 1, 0)

    vec = pl.BlockSpec((1, cw), lambda t: (0, conv(t) % nblk))
    stage = (cw // CONV_GROUP_DIM, seq + 2 * CONV_HALO, CONV_GROUP_DIM)
    return pl.pallas_call(
        functools.partial(_conv_branch_kernel, seq=seq),
        grid=(items + 1,),
        in_specs=[
            pl.BlockSpec((None, seq, d), lambda t: (proj(t) // nblk, 0, 0)),
            pl.BlockSpec((None, d, cw), lambda t: (idx, 0, proj(t) % nblk)),
            pl.BlockSpec((None, d, cw), lambda t: (idx, 0, nblk + proj(t) % nblk)),
            pl.BlockSpec((CONV_WIDTH, cw), lambda t: (0, conv(t) % nblk)),
            vec, vec, vec,
        ],
        out_specs=pl.BlockSpec((None, seq, cw), lambda t: (conv(t) // nblk, 0, conv(t) % nblk)),
        out_shape=jax.ShapeDtypeStruct((bsz, seq, cdim), BF16),
        scratch_shapes=[pltpu.VMEM(stage, F32), pltpu.VMEM(stage, F32), pltpu.VMEM((CONV_WIDTH * SUBLANES, cw), F32)],
        compiler_params=_params("arbitrary"),
        name="conv_branch",
    )(h, w_in, w_in, dw_w, dw_b.reshape(1, cdim), ln_g.reshape(1, cdim), ln_b.reshape(1, cdim))


def _proj_residual_kernel(*refs, n_lhs, has_bias):
    lhs_refs = refs[:n_lhs]
    w_ref, x_ref = refs[n_lhs], refs[n_lhs + 1]
    o_ref, wb_ref = refs[-2], refs[-1]

    @pl.when(pl.program_id(0) == 0)
    def _():
        _cast_weight(w_ref, wb_ref)

    acc = x_ref[...]
    if has_bias:
        acc = acc + refs[n_lhs + 2][...]
    k0 = 0
    for lhs_ref in lhs_refs:
        kw = lhs_ref.shape[-1]
        acc = acc + jnp.dot(lhs_ref[...], wb_ref[k0:k0 + kw, :], preferred_element_type=F32)
        k0 += kw
    o_ref[...] = acc


def _proj_residual(lhs_list, w, idx, x, bias=None, *, tm):
    m, n = x.shape
    k = w.shape[1]
    in_specs = [pl.BlockSpec((tm, lhs.shape[1]), lambda i: (i, 0)) for lhs in lhs_list]
    in_specs += [pl.BlockSpec((None, k, n), lambda i: (idx, 0, 0), pipeline_mode=pl.Buffered(1)),
                 pl.BlockSpec((tm, n), lambda i: (i, 0))]
    args = list(lhs_list) + [w, x]
    if bias is not None:
        in_specs.append(pl.BlockSpec((1, n), lambda i: (0, 0)))
        args.append(bias.reshape(1, n))
    return pl.pallas_call(
        functools.partial(_proj_residual_kernel, n_lhs=len(lhs_list), has_bias=bias is not None),
        grid=(m // tm,),
        in_specs=in_specs,
        out_specs=pl.BlockSpec((tm, n), lambda i: (i, 0)),
        out_shape=jax.ShapeDtypeStruct((m, n), F32),
        scratch_shapes=[pltpu.VMEM((k, n), BF16)],
        compiler_params=_params("arbitrary"),
        name="proj_residual",
    )(*args)


def _ffn_kernel(x_ref, g_ref, wg_ref, wu_ref, wd_ref, fg_ref, o_ref, h_ref, *, final_norm):
    f = pl.program_id(1)

    @pl.when(f == 0)
    def _():
        x = x_ref[...]
        h_ref[...] = _rms(x, g_ref[...]).astype(BF16)
        o_ref[...] = x

    tf = wg_ref.shape[-1]
    w_gu = jnp.concatenate([wg_ref[...].astype(BF16), wu_ref[...].astype(BF16)], axis=1)
    gu = jnp.dot(h_ref[...], w_gu, preferred_element_type=F32)
    act = (_silu(gu[:, :tf]) * gu[:, tf:]).astype(BF16)
    o_ref[...] += jnp.dot(act, wd_ref[...].astype(BF16), preferred_element_type=F32)

    if final_norm:
        @pl.when(f == pl.num_programs(1) - 1)
        def _():
            o_ref[...] = _rms(o_ref[...], fg_ref[...])


def _ffn(x, g, w_gate, w_up, w_down, idx, final_g, *, tm, tf, final_norm):
    m, d = x.shape
    dff = w_gate.shape[2]
    return pl.pallas_call(
        functools.partial(_ffn_kernel, final_norm=final_norm),
        grid=(m // tm, dff // tf),
        in_specs=[
            pl.BlockSpec((tm, d), lambda i, f: (i, 0)),
            pl.BlockSpec((1, d), lambda i, f: (0, 0)),
            pl.BlockSpec((None, d, tf), lambda i, f: (idx, 0, f)),
            pl.BlockSpec((None, d, tf), lambda i, f: (idx, 0, f)),
            pl.BlockSpec((None, tf, d), lambda i, f: (idx, f, 0)),
            pl.BlockSpec((1, d), lambda i, f: (0, 0)),
        ],
        out_specs=pl.BlockSpec((tm, d), lambda i, f: (i, 0)),
        out_shape=jax.ShapeDtypeStruct((m, d), F32),
        scratch_shapes=[pltpu.VMEM((tm, d), BF16)],
        compiler_params=_params("parallel", "arbitrary"),
        name="ffn",
    )(x, g.reshape(1, d), w_gate, w_up, w_down, final_g.reshape(1, d))


def _fnet_channel_kernel(x_ref, g_ref, cs_ref, a_ref, b_ref):
    h = _rms(x_ref[...], g_ref[...]).astype(BF16)
    gd = cs_ref.shape[0]
    for grp in range(FNET_GROUPS):
        cols = slice(grp * gd, (grp + 1) * gd)
        ab = jnp.dot(h[:, cols], cs_ref[...], preferred_element_type=F32)
        a_ref[:, cols] = ab[:, :gd].astype(a_ref.dtype)
        b_ref[:, cols] = ab[:, gd:].astype(b_ref.dtype)


def _fnet_channel(x, g, cs, *, tm):
    m, d = x.shape
    gd = cs.shape[0]
    out = jax.ShapeDtypeStruct((m, d), BF16)
    return pl.pallas_call(
        _fnet_channel_kernel,
        grid=(m // tm,),
        in_specs=[
            pl.BlockSpec((tm, d), lambda i: (i, 0)),
            pl.BlockSpec((1, d), lambda i: (0, 0)),
            pl.BlockSpec((gd, 2 * gd), lambda i: (0, 0)),
        ],
        out_specs=[pl.BlockSpec((tm, d), lambda i: (i, 0))] * 2,
        out_shape=[out, out],
        compiler_params=_params("parallel"),
        name="fnet_channel",
    )(x, g.reshape(1, d), cs)


def _fnet_seq_kernel(t_ref, p_ref, a_ref, b_ref, o_ref, uv_ref, y_ref):
    q = t_ref.shape[1]

    def fold(rows):
        a0, a1, a2, a3 = [a_ref[pl.ds(m * q + rows.start, rows.size), :].astype(F32) for m in range(FFT_RADIX)]
        b0, b1, b2, b3 = [b_ref[pl.ds(m * q + rows.start, rows.size), :].astype(F32) for m in range(FFT_RADIX)]
        ae, ao, ad, aq = a0 + a2, a1 + a3, a0 - a2, a1 - a3
        be, bo, bd, bq = b0 + b2, b1 + b3, b0 - b2, b1 - b3
        u = (ae + ao, ad - bq, ae - ao, ad + bq)
        v = (be + bo, bd + aq, be - bo, bd - aq)
        for r in range(FFT_RADIX):
            uv_ref[r, rows, :] = u[r].astype(BF16)
            uv_ref[r, pl.ds(q + rows.start, rows.size), :] = v[r].astype(BF16)

    _for_row_chunks(q, fold)
    for r in range(FFT_RADIX):
        y_ref[r] = jnp.dot(t_ref[r], uv_ref[r], preferred_element_type=F32).astype(BF16)
    rows = q // FFT_RADIX
    for kt in range(FFT_RADIX):
        blk = jnp.concatenate([y_ref[r, kt * rows:(kt + 1) * rows, :] for r in range(FFT_RADIX)], axis=0)
        o_ref[kt * q:(kt + 1) * q, :] = jnp.dot(p_ref[...], blk, preferred_element_type=F32).astype(o_ref.dtype)


def _fnet_seq(tables, perm, a, b, *, tn):
    bsz, seq, d = a.shape
    q = seq // FFT_RADIX
    return pl.pallas_call(
        _fnet_seq_kernel,
        grid=(bsz, d // tn),
        in_specs=[
            pl.BlockSpec((FFT_RADIX, q, 2 * q), lambda bb, j: (0, 0, 0)),
            pl.BlockSpec((q, q), lambda bb, j: (0, 0)),
            pl.BlockSpec((None, seq, tn), lambda bb, j: (bb, 0, j)),
            pl.BlockSpec((None, seq, tn), lambda bb, j: (bb, 0, j)),
        ],
        out_specs=pl.BlockSpec((None, seq, tn), lambda bb, j: (bb, 0, j)),
        out_shape=jax.ShapeDtypeStruct((bsz, seq, d), BF16),
        scratch_shapes=[pltpu.VMEM((FFT_RADIX, 2 * q, tn), BF16), pltpu.VMEM((FFT_RADIX, q, tn), BF16)],
        compiler_params=_params("parallel", "parallel"),
        name="fnet_seq",
    )(tables, perm, a, b)


def _interleave_matrix(q):
    dst = jnp.arange(q, dtype=jnp.int32)
    src = (dst % FFT_RADIX) * (q // FFT_RADIX) + dst // FFT_RADIX
    return (src[:, None] == jnp.arange(q, dtype=jnp.int32)[None, :]).astype(BF16)


def _cos_sin(rows, cols, n):
    ang = ((rows[:, None] * cols[None, :]) % n).astype(F32) * (2.0 * math.pi / n)
    return jnp.cos(ang), jnp.sin(ang)


def _dft_tables(n, scale):
    q = 1 << (int(math.log2(n)) // 2)
    cols = jnp.arange(n, dtype=jnp.int32)
    ca, sa = _cos_sin(jnp.arange(n // q, dtype=jnp.int32) * q, cols, n)
    cb, sb = _cos_sin(jnp.arange(q, dtype=jnp.int32), cols, n)
    ca, sa = (ca * scale)[:, None, :], (sa * scale)[:, None, :]
    cb, sb = cb[None, :, :], sb[None, :, :]
    return (ca * cb - sa * sb).reshape(n, n), (sa * cb + ca * sb).reshape(n, n)


def _radix_tables(n):
    q = n // FFT_RADIX
    c4, s4 = _dft_tables(q, 1.0 / math.sqrt(n))
    cr, sr = _cos_sin(jnp.arange(FFT_RADIX, dtype=jnp.int32), jnp.arange(q, dtype=jnp.int32), n)
    cr, sr = cr[:, None, :], sr[:, None, :]
    c = c4[None] * cr - s4[None] * sr
    sn = s4[None] * cr + c4[None] * sr
    return jnp.concatenate([c, -sn], axis=2)


def kernel(x, mix_norm_g, ffn_norm_g, final_norm_g, ab_w_in, conv_dw_w, conv_dw_b, conv_ln_g, conv_ln_b,
           sgu_ln_g, sgu_ln_b, sgu_w, sgu_b, ab_w_out, fnet_w_out, fnet_b_out, ffn_w_gate, ffn_w_up, ffn_w_down):
    bsz, seq, d = x.shape
    m = bsz * seq
    depth = mix_norm_g.shape[0]
    xf = x.reshape(m, d)
    for layer in range(depth):
        if layer % 2 == 0:
            i = layer // 2
            y_sgu, h = _sgu_branch(xf, mix_norm_g[layer], ab_w_in, i, sgu_ln_g[i], sgu_ln_b[i], sgu_w[i], sgu_b[i],
                                   tm=512)
            y_conv = _conv_branch(h.reshape(bsz, seq, d), ab_w_in, i, conv_dw_w[i], conv_dw_b[i], conv_ln_g[i],
                                  conv_ln_b[i])
            xf = _proj_residual([y_conv.reshape(m, -1), y_sgu], ab_w_out, i, xf, tm=512)
        else:
            j = layer // 2
            gd = d // FNET_GROUPS
            cc, sc = _dft_tables(gd, 1.0 / math.sqrt(gd))
            a, b = _fnet_channel(xf, mix_norm_g[layer], jnp.concatenate([cc, sc], axis=1).astype(BF16), tm=512)
            y = _fnet_seq(_radix_tables(seq).astype(BF16), _interleave_matrix(seq // FFT_RADIX),
                          a.reshape(bsz, seq, d), b.reshape(bsz, seq, d), tn=1024)
            xf = _proj_residual([y.reshape(m, d)], fnet_w_out, j, xf, fnet_b_out[j], tm=512)
        xf = _ffn(xf, ffn_norm_g[layer], ffn_w_gate, ffn_w_up, ffn_w_down, layer, final_norm_g,
                  tm=1024, tf=256, final_norm=layer == depth - 1)
    return xf.reshape(bsz, seq, d)
```

```python
import functools
import math

import jax
import jax.numpy as jnp
from jax import lax
from jax.experimental import pallas as pl
from jax.experimental.pallas import tpu as pltpu

F32 = jnp.float32
BF16 = jnp.bfloat16

RMS_EPS = 1e-6
LN_EPS = 1e-5
CONV_GROUP_DIM = 128
CONV_WIDTH = 31
CONV_PAD = (CONV_WIDTH - 1) // 2
CONV_HALO = 16
CONV_ROWS = 64
CONV_BLOCK = 256
SUBLANES = 8
SGU_HEADS = 8
CHUNK = 128
FNET_GROUPS = 8
FFT_RADIX = 4

V7X_VMEM_BYTES = 64 * 1024 * 1024
VMEM_LIMIT_BYTES = V7X_VMEM_BYTES - 3 * 1024 * 1024
ROW_CHUNK = 256


def _params(*semantics, flags=None):
    return pltpu.CompilerParams(dimension_semantics=semantics, vmem_limit_bytes=VMEM_LIMIT_BYTES, flags=flags)


def _rms(x, g):
    return x * lax.rsqrt(jnp.mean(x * x, axis=-1, keepdims=True) + RMS_EPS) * g


def _layer_norm(x, g, b):
    mu = jnp.mean(x, axis=-1, keepdims=True)
    xc = x - mu
    return xc * lax.rsqrt(jnp.mean(xc * xc, axis=-1, keepdims=True) + LN_EPS) * g + b


def _gelu(x):
    return 0.5 * x * (1.0 + lax.erf(x * (1.0 / math.sqrt(2.0))))


def _silu(x):
    return x * jax.nn.sigmoid(x)


def _for_row_chunks(nrows, fn):
    def body(r, carry):
        fn(pl.ds(pl.multiple_of(r * ROW_CHUNK, ROW_CHUNK), ROW_CHUNK))
        return carry
    lax.fori_loop(0, nrows // ROW_CHUNK, body, 0)


def _cast_weight(w_ref, wb_ref):
    def cast(rows):
        wb_ref[rows, :] = w_ref[rows, :].astype(BF16)
    _for_row_chunks(w_ref.shape[0], cast)


def _sgu_branch_kernel(x_ref, g_ref, w_ref, lg_ref, lb_ref, ws_ref, bs_ref, o_ref, h_ref, wb_ref, z_a, z_b, *, tm, nt):
    t = pl.program_id(0)
    gdim = o_ref.shape[-1]
    hd = gdim // SGU_HEADS

    def project(z_w):
        for r in range(tm // ROW_CHUNK):
            rows = slice(r * ROW_CHUNK, (r + 1) * ROW_CHUNK)
            h = _rms(x_ref[rows, :], g_ref[...]).astype(BF16)
            h_ref[rows, :] = h
            z_w[rows, :] = jnp.dot(h, wb_ref[...], preferred_element_type=F32)

    def gate(z_r):
        for c in range(tm // CHUNK):
            rows = slice(c * CHUNK, (c + 1) * CHUNK)
            v = _layer_norm(_gelu(z_r[rows, gdim:]), lg_ref[...], lb_ref[...]).astype(BF16)
            for hh in range(SGU_HEADS):
                cols = slice(hh * hd, (hh + 1) * hd)
                mixed = jnp.dot(ws_ref[hh].astype(BF16), v[:, cols], preferred_element_type=F32)
                mixed = mixed + bs_ref[:, hh:hh + 1]
                o_ref[rows, cols] = (_gelu(z_r[rows, cols]) * mixed).astype(o_ref.dtype)

    @pl.when(t == 0)
    def _():
        _cast_weight(w_ref, wb_ref)
        project(z_a)

    for parity, (z_w, z_r) in enumerate(((z_a, z_b), (z_b, z_a))):
        @pl.when((t > 0) & (t < nt) & (t % 2 == parity))
        def _():
            project(z_w)
            gate(z_r)

    @pl.when(t == nt)
    def _():
        gate(z_a if nt % 2 == 1 else z_b)


def _sgu_branch(x, g, w_in, idx, ln_g, ln_b, w_s, b_s, *, tm):
    m, d = x.shape
    gdim = ln_g.shape[0]
    wblk = w_in.shape[2] // (2 * gdim) - 1
    nt = m // tm

    def proj(t):
        return jnp.minimum(t, nt - 1)

    def gated(t):
        return jnp.maximum(t - 1, 0)

    return pl.pallas_call(
        functools.partial(_sgu_branch_kernel, tm=tm, nt=nt),
        grid=(nt + 1,),
        in_specs=[
            pl.BlockSpec((tm, d), lambda t: (proj(t), 0)),
            pl.BlockSpec((1, d), lambda t: (0, 0)),
            pl.BlockSpec((None, d, 2 * gdim), lambda t: (idx, 0, wblk), pipeline_mode=pl.Buffered(1)),
            pl.BlockSpec((1, gdim), lambda t: (0, 0)),
            pl.BlockSpec((1, gdim), lambda t: (0, 0)),
            pl.BlockSpec((SGU_HEADS, CHUNK, CHUNK), lambda t: (0, 0, 0)),
            pl.BlockSpec((CHUNK, SGU_HEADS), lambda t: (0, 0)),
        ],
        out_specs=[pl.BlockSpec((tm, gdim), lambda t: (gated(t), 0)), pl.BlockSpec((tm, d), lambda t: (proj(t), 0))],
        out_shape=[jax.ShapeDtypeStruct((m, gdim), BF16), jax.ShapeDtypeStruct((m, d), BF16)],
        scratch_shapes=[pltpu.VMEM((d, 2 * gdim), BF16), pltpu.VMEM((tm, 2 * gdim), F32),
                        pltpu.VMEM((tm, 2 * gdim), F32)],
        compiler_params=_params("arbitrary"),
        name="sgu_branch",
    )(x, g.reshape(1, d), w_in, ln_g.reshape(1, gdim), ln_b.reshape(1, gdim), w_s, b_s.T)


def _conv_step(h_ref, wa_ref, wg_ref, dwb_ref, b_ref, lg_ref, lb_ref, o_ref, stage_w, stage_r, *, seq):
    cw = o_ref.shape[-1]
    gd = CONV_GROUP_DIM
    w = jnp.concatenate([wa_ref[...].astype(BF16), wg_ref[...].astype(BF16)], axis=1)

    def glu(c):
        z = jnp.dot(h_ref[c * ROW_CHUNK:(c + 1) * ROW_CHUNK, :], w, preferred_element_type=F32)
        start = CONV_HALO + c * ROW_CHUNK
        u = z[:, :cw] * jax.nn.sigmoid(z[:, cw:])
        for grp in range(cw // gd):
            stage_w[grp, start:start + ROW_CHUNK, :] = u[:, grp * gd:(grp + 1) * gd]

    def conv(c):
        for s in range(ROW_CHUNK // CONV_ROWS):
            r0 = c * ROW_CHUNK + s * CONV_ROWS
            base = r0 + CONV_HALO - CONV_PAD
            for grp in range(cw // gd):
                cols = slice(grp * gd, (grp + 1) * gd)
                acc = jnp.zeros((CONV_ROWS // SUBLANES, SUBLANES, gd), F32)
                for k in range(CONV_WIDTH):
                    tap = stage_r[grp, base + k:base + k + CONV_ROWS, :].reshape(CONV_ROWS // SUBLANES, SUBLANES, gd)
                    acc = acc + tap * dwb_ref[k * SUBLANES:(k + 1) * SUBLANES, cols]
                y = _layer_norm(acc.reshape(CONV_ROWS, gd) + b_ref[:, cols], lg_ref[:, cols], lb_ref[:, cols])
                o_ref[r0:r0 + CONV_ROWS, cols] = _silu(y).astype(o_ref.dtype)

    for c in range(seq // ROW_CHUNK):
        glu(c)
        conv(c)


def _conv_branch_kernel(h_ref, wa_ref, wg_ref, dw_ref, b_ref, lg_ref, lb_ref, o_ref, stage_a, stage_b, dwb_ref, *, seq):
    t = pl.program_id(0)
    cw = o_ref.shape[-1]

    @pl.when(t == 0)
    def _():
        stage_a[...] = jnp.zeros_like(stage_a)
        stage_b[...] = jnp.zeros_like(stage_b)

    for k in range(CONV_WIDTH):
        dwb_ref[k * SUBLANES:(k + 1) * SUBLANES, :] = jnp.broadcast_to(dw_ref[k:k + 1, :], (SUBLANES, cw))

    step = functools.partial(_conv_step, h_ref, wa_ref, wg_ref, dwb_ref, b_ref, lg_ref, lb_ref, o_ref, seq=seq)

    @pl.when(t % 2 == 0)
    def _():
        step(stage_a, stage_b)

    @pl.when(t % 2 == 1)
    def _():
        step(stage_b, stage_a)


def _conv_branch(h, w_in, idx, dw_w, dw_b, ln_g, ln_b):
    bsz, seq, d = h.shape
    cdim = dw_w.shape[1]
    cw = CONV_BLOCK
    nblk = cdim // cw
    items = bsz * nblk

    def proj(t):
        return jnp.minimum(t, items - 1)

    def conv(t):
        return jnp.maximum(t - 1, 0)

    vec = pl.BlockSpec((1, cw), lambda t: (0, conv(t) % nblk))
    stage = (cw // CONV_GROUP_DIM, seq + 2 * CONV_HALO, CONV_GROUP_DIM)
    return pl.pallas_call(
        functools.partial(_conv_branch_kernel, seq=seq),
        grid=(items + 1,),
        in_specs=[
            pl.BlockSpec((None, seq, d), lambda t: (proj(t) // nblk, 0, 0)),
            pl.BlockSpec((None, d, cw), lambda t: (idx, 0, proj(t) % nblk)),
            pl.BlockSpec((None, d, cw), lambda t: (idx, 0, nblk + proj(t) % nblk)),
            pl.BlockSpec((CONV_WIDTH, cw), lambda t: (0, conv(t) % nblk)),
            vec, vec, vec,
        ],
        out_specs=pl.BlockSpec((None, seq, cw), lambda t: (conv(t) // nblk, 0, conv(t) % nblk)),
        out_shape=jax.ShapeDtypeStruct((bsz, seq, cdim), BF16),
        scratch_shapes=[pltpu.VMEM(stage, F32), pltpu.VMEM(stage, F32), pltpu.VMEM((CONV_WIDTH * SUBLANES, cw), F32)],
        compiler_params=_params("arbitrary"),
        name="conv_branch",
    )(h, w_in, w_in, dw_w, dw_b.reshape(1, cdim), ln_g.reshape(1, cdim), ln_b.reshape(1, cdim))


def _proj_residual_kernel(*refs, n_lhs, has_bias):
    lhs_refs = refs[:n_lhs]
    w_ref, x_ref = refs[n_lhs], refs[n_lhs + 1]
    o_ref, wb_ref = refs[-2], refs[-1]

    @pl.when(pl.program_id(0) == 0)
    def _():
        _cast_weight(w_ref, wb_ref)

    acc = x_ref[...]
    if has_bias:
        acc = acc + refs[n_lhs + 2][...]
    k0 = 0
    for lhs_ref in lhs_refs:
        kw = lhs_ref.shape[-1]
        acc = acc + jnp.dot(lhs_ref[...], wb_ref[k0:k0 + kw, :], preferred_element_type=F32)
        k0 += kw
    o_ref[...] = acc


def _proj_residual(lhs_list, w, idx, x, bias=None, *, tm):
    m, n = x.shape
    k = w.shape[1]
    in_specs = [pl.BlockSpec((tm, lhs.shape[1]), lambda i: (i, 0)) for lhs in lhs_list]
    in_specs += [pl.BlockSpec((None, k, n), lambda i: (idx, 0, 0), pipeline_mode=pl.Buffered(1)),
                 pl.BlockSpec((tm, n), lambda i: (i, 0))]
    args = list(lhs_list) + [w, x]
    if bias is not None:
        in_specs.append(pl.BlockSpec((1, n), lambda i: (0, 0)))
        args.append(bias.reshape(1, n))
    return pl.pallas_call(
        functools.partial(_proj_residual_kernel, n_lhs=len(lhs_list), has_bias=bias is not None),
        grid=(m // tm,),
        in_specs=in_specs,
        out_specs=pl.BlockSpec((tm, n), lambda i: (i, 0)),
        out_shape=jax.ShapeDtypeStruct((m, n), F32),
        scratch_shapes=[pltpu.VMEM((k, n), BF16)],
        compiler_params=_params("arbitrary"),
        name="proj_residual",
    )(*args)


def _ffn_kernel(x_hbm, g_ref, wg_ref, wu_ref, wd_ref, fg_ref, o_ref, h_ref, x_buf, x_sem, *, final_norm):
    i = pl.program_id(0)
    f = pl.program_id(1)
    tm = x_buf.shape[0]

    def x_copy(tile):
        return pltpu.make_async_copy(x_hbm.at[pl.ds(pl.multiple_of(tile * tm, tm), tm), :], x_buf, x_sem)

    @pl.when((i == 0) & (f == 0))
    def _():
        x_copy(0).start()

    @pl.when(f == 0)
    def _():
        x_copy(i).wait()
        x = x_buf[...]
        h_ref[...] = _rms(x, g_ref[...]).astype(BF16)
        o_ref[...] = x

    @pl.when((f == 1) & (i + 1 < pl.num_programs(0)))
    def _():
        x_copy(i + 1).start()

    tf = wg_ref.shape[-1]
    w_gu = jnp.concatenate([wg_ref[...].astype(BF16), wu_ref[...].astype(BF16)], axis=1)
    gu = jnp.dot(h_ref[...], w_gu, preferred_element_type=F32)
    act = (_silu(gu[:, :tf]) * gu[:, tf:]).astype(BF16)
    o_ref[...] += jnp.dot(act, wd_ref[...].astype(BF16), preferred_element_type=F32)

    if final_norm:
        @pl.when(f == pl.num_programs(1) - 1)
        def _():
            o_ref[...] = _rms(o_ref[...], fg_ref[...])


def _ffn(x, g, w_gate, w_up, w_down, idx, final_g, *, tm, tf, final_norm):
    m, d = x.shape
    dff = w_gate.shape[2]
    return pl.pallas_call(
        functools.partial(_ffn_kernel, final_norm=final_norm),
        grid=(m // tm, dff // tf),
        in_specs=[
            pl.BlockSpec(memory_space=pl.ANY),
            pl.BlockSpec((1, d), lambda i, f: (0, 0)),
            pl.BlockSpec((None, d, tf), lambda i, f: (idx, 0, f)),
            pl.BlockSpec((None, d, tf), lambda i, f: (idx, 0, f)),
            pl.BlockSpec((None, tf, d), lambda i, f: (idx, f, 0)),
            pl.BlockSpec((1, d), lambda i, f: (0, 0)),
        ],
        out_specs=pl.BlockSpec((tm, d), lambda i, f: (i, 0)),
        out_shape=jax.ShapeDtypeStruct((m, d), F32),
        scratch_shapes=[pltpu.VMEM((tm, d), BF16), pltpu.VMEM((tm, d), F32), pltpu.SemaphoreType.DMA(())],
        compiler_params=_params("arbitrary", "arbitrary"),
        name="ffn",
    )(x, g.reshape(1, d), w_gate, w_up, w_down, final_g.reshape(1, d))


def _fnet_channel_kernel(x_ref, g_ref, cs_ref, a_ref, b_ref):
    h = _rms(x_ref[...], g_ref[...]).astype(BF16)
    gd = cs_ref.shape[0]
    for grp in range(FNET_GROUPS):
        cols = slice(grp * gd, (grp + 1) * gd)
        ab = jnp.dot(h[:, cols], cs_ref[...], preferred_element_type=F32)
        a_ref[:, cols] = ab[:, :gd].astype(a_ref.dtype)
        b_ref[:, cols] = ab[:, gd:].astype(b_ref.dtype)


def _fnet_channel(x, g, cs, *, tm):
    m, d = x.shape
    gd = cs.shape[0]
    out = jax.ShapeDtypeStruct((m, d), BF16)
    return pl.pallas_call(
        _fnet_channel_kernel,
        grid=(m // tm,),
        in_specs=[
            pl.BlockSpec((tm, d), lambda i: (i, 0)),
            pl.BlockSpec((1, d), lambda i: (0, 0)),
            pl.BlockSpec((gd, 2 * gd), lambda i: (0, 0)),
        ],
        out_specs=[pl.BlockSpec((tm, d), lambda i: (i, 0))] * 2,
        out_shape=[out, out],
        compiler_params=_params("parallel"),
        name="fnet_channel",
    )(x, g.reshape(1, d), cs)


def _fnet_seq_kernel(t_ref, p_ref, a_ref, b_ref, o_ref, uv_ref, y_ref):
    q = t_ref.shape[1]

    def fold(rows):
        a0, a1, a2, a3 = [a_ref[pl.ds(m * q + rows.start, rows.size), :].astype(F32) for m in range(FFT_RADIX)]
        b0, b1, b2, b3 = [b_ref[pl.ds(m * q + rows.start, rows.size), :].astype(F32) for m in range(FFT_RADIX)]
        ae, ao, ad, aq = a0 + a2, a1 + a3, a0 - a2, a1 - a3
        be, bo, bd, bq = b0 + b2, b1 + b3, b0 - b2, b1 - b3
        u = (ae + ao, ad - bq, ae - ao, ad + bq)
        v = (be + bo, bd + aq, be - bo, bd - aq)
        for r in range(FFT_RADIX):
            uv_ref[r, rows, :] = u[r].astype(BF16)
            uv_ref[r, pl.ds(q + rows.start, rows.size), :] = v[r].astype(BF16)

    _for_row_chunks(q, fold)
    for r in range(FFT_RADIX):
        y_ref[r] = jnp.dot(t_ref[r], uv_ref[r], preferred_element_type=F32).astype(BF16)
    rows = q // FFT_RADIX
    for kt in range(FFT_RADIX):
        blk = jnp.concatenate([y_ref[r, kt * rows:(kt + 1) * rows, :] for r in range(FFT_RADIX)], axis=0)
        o_ref[kt * q:(kt + 1) * q, :] = jnp.dot(p_ref[...], blk, preferred_element_type=F32).astype(o_ref.dtype)


def _fnet_seq(tables, perm, a, b, *, tn):
    bsz, seq, d = a.shape
    q = seq // FFT_RADIX
    return pl.pallas_call(
        _fnet_seq_kernel,
        grid=(bsz, d // tn),
        in_specs=[
            pl.BlockSpec((FFT_RADIX, q, 2 * q), lambda bb, j: (0, 0, 0)),
            pl.BlockSpec((q, q), lambda bb, j: (0, 0)),
            pl.BlockSpec((None, seq, tn), lambda bb, j: (bb, 0, j)),
            pl.BlockSpec((None, seq, tn), lambda bb, j: (bb, 0, j)),
        ],
        out_specs=pl.BlockSpec((None, seq, tn), lambda bb, j: (bb, 0, j)),
        out_shape=jax.ShapeDtypeStruct((bsz, seq, d), BF16),
        scratch_shapes=[pltpu.VMEM((FFT_RADIX, 2 * q, tn), BF16), pltpu.VMEM((FFT_RADIX, q, tn), BF16)],
        compiler_params=_params("parallel", "parallel"),
        name="fnet_seq",
    )(tables, perm, a, b)


def _interleave_matrix(q):
    dst = jnp.arange(q, dtype=jnp.int32)
    src = (dst % FFT_RADIX) * (q // FFT_RADIX) + dst // FFT_RADIX
    return (src[:, None] == jnp.arange(q, dtype=jnp.int32)[None, :]).astype(BF16)


def _cos_sin(rows, cols, n):
    ang = ((rows[:, None] * cols[None, :]) % n).astype(F32) * (2.0 * math.pi / n)
    return jnp.cos(ang), jnp.sin(ang)


def _dft_tables(n, scale):
    q = 1 << (int(math.log2(n)) // 2)
    cols = jnp.arange(n, dtype=jnp.int32)
    ca, sa = _cos_sin(jnp.arange(n // q, dtype=jnp.int32) * q, cols, n)
    cb, sb = _cos_sin(jnp.arange(q, dtype=jnp.int32), cols, n)
    ca, sa = (ca * scale)[:, None, :], (sa * scale)[:, None, :]
    cb, sb = cb[None, :, :], sb[None, :, :]
    return (ca * cb - sa * sb).reshape(n, n), (sa * cb + ca * sb).reshape(n, n)


def _radix_tables(n):
    q = n // FFT_RADIX
    c4, s4 = _dft_tables(q, 1.0 / math.sqrt(n))
    cr, sr = _cos_sin(jnp.arange(FFT_RADIX, dtype=jnp.int32), jnp.arange(q, dtype=jnp.int32), n)
    cr, sr = cr[:, None, :], sr[:, None, :]
    c = c4[None] * cr - s4[None] * sr
    sn = s4[None] * cr + c4[None] * sr
    return jnp.concatenate([c, -sn], axis=2)


def kernel(x, mix_norm_g, ffn_norm_g, final_norm_g, ab_w_in, conv_dw_w, conv_dw_b, conv_ln_g, conv_ln_b,
           sgu_ln_g, sgu_ln_b, sgu_w, sgu_b, ab_w_out, fnet_w_out, fnet_b_out, ffn_w_gate, ffn_w_up, ffn_w_down):
    bsz, seq, d = x.shape
    m = bsz * seq
    depth = mix_norm_g.shape[0]
    xf = x.reshape(m, d)
    for layer in range(depth):
        if layer % 2 == 0:
            i = layer // 2
            y_sgu, h = _sgu_branch(xf, mix_norm_g[layer], ab_w_in, i, sgu_ln_g[i], sgu_ln_b[i], sgu_w[i], sgu_b[i],
                                   tm=512)
            y_conv = _conv_branch(h.reshape(bsz, seq, d), ab_w_in, i, conv_dw_w[i], conv_dw_b[i], conv_ln_g[i],
                                  conv_ln_b[i])
            xf = _proj_residual([y_conv.reshape(m, -1), y_sgu], ab_w_out, i, xf, tm=512)
        else:
            j = layer // 2
            gd = d // FNET_GROUPS
            cc, sc = _dft_tables(gd, 1.0 / math.sqrt(gd))
            a, b = _fnet_channel(xf, mix_norm_g[layer], jnp.concatenate([cc, sc], axis=1).astype(BF16), tm=512)
            y = _fnet_seq(_radix_tables(seq).astype(BF16), _interleave_matrix(seq // FFT_RADIX),
                          a.reshape(bsz, seq, d), b.reshape(bsz, seq, d), tn=1024)
            xf = _proj_residual([y.reshape(m, d)], fnet_w_out, j, xf, fnet_b_out[j], tm=512)
        xf = _ffn(xf, ffn_norm_g[layer], ffn_w_gate, ffn_w_up, ffn_w_down, layer, final_norm_g,
                  tm=1024, tf=512, final_norm=layer == depth - 1)
    return xf.reshape(bsz, seq, d)
```

```python
import functools
import math

import jax
import jax.numpy as jnp
from jax import lax
from jax.experimental import pallas as pl
from jax.experimental.pallas import tpu as pltpu

F32 = jnp.float32
BF16 = jnp.bfloat16

RMS_EPS = 1e-6
LN_EPS = 1e-5
CONV_GROUP_DIM = 128
CONV_WIDTH = 31
CONV_PAD = (CONV_WIDTH - 1) // 2
CONV_HALO = 16
CONV_ROWS = 64
CONV_BLOCK = 256
SUBLANES = 8
SGU_HEADS = 8
CHUNK = 128
FNET_GROUPS = 8
FFT_RADIX = 4

V7X_VMEM_BYTES = 64 * 1024 * 1024
VMEM_LIMIT_BYTES = V7X_VMEM_BYTES - 3 * 1024 * 1024
ROW_CHUNK = 256


def _params(*semantics, flags=None):
    return pltpu.CompilerParams(dimension_semantics=semantics, vmem_limit_bytes=VMEM_LIMIT_BYTES, flags=flags)


def _rms(x, g):
    return x * lax.rsqrt(jnp.mean(x * x, axis=-1, keepdims=True) + RMS_EPS) * g


def _layer_norm(x, g, b):
    mu = jnp.mean(x, axis=-1, keepdims=True)
    xc = x - mu
    return xc * lax.rsqrt(jnp.mean(xc * xc, axis=-1, keepdims=True) + LN_EPS) * g + b


def _gelu(x):
    return 0.5 * x * (1.0 + lax.erf(x * (1.0 / math.sqrt(2.0))))


def _silu(x):
    return x * jax.nn.sigmoid(x)


def _for_row_chunks(nrows, fn):
    def body(r, carry):
        fn(pl.ds(pl.multiple_of(r * ROW_CHUNK, ROW_CHUNK), ROW_CHUNK))
        return carry
    lax.fori_loop(0, nrows // ROW_CHUNK, body, 0)


def _cast_weight(w_ref, wb_ref):
    def cast(rows):
        wb_ref[rows, :] = w_ref[rows, :].astype(BF16)
    _for_row_chunks(w_ref.shape[0], cast)


def _sgu_branch_kernel(x_ref, g_ref, w_ref, lg_ref, lb_ref, ws_ref, bs_ref, o_ref, h_ref, wb_ref, z_a, z_b, *, tm, nt):
    t = pl.program_id(0)
    gdim = o_ref.shape[-1]
    hd = gdim // SGU_HEADS

    def project(z_w):
        for r in range(tm // ROW_CHUNK):
            rows = slice(r * ROW_CHUNK, (r + 1) * ROW_CHUNK)
            h = _rms(x_ref[rows, :], g_ref[...]).astype(BF16)
            h_ref[rows, :] = h
            z_w[rows, :] = jnp.dot(h, wb_ref[...], preferred_element_type=F32)

    def gate(z_r):
        for c in range(tm // CHUNK):
            rows = slice(c * CHUNK, (c + 1) * CHUNK)
            v = _layer_norm(_gelu(z_r[rows, gdim:]), lg_ref[...], lb_ref[...]).astype(BF16)
            for hh in range(SGU_HEADS):
                cols = slice(hh * hd, (hh + 1) * hd)
                mixed = jnp.dot(ws_ref[hh].astype(BF16), v[:, cols], preferred_element_type=F32)
                mixed = mixed + bs_ref[:, hh:hh + 1]
                o_ref[rows, cols] = (_gelu(z_r[rows, cols]) * mixed).astype(o_ref.dtype)

    @pl.when(t == 0)
    def _():
        _cast_weight(w_ref, wb_ref)
        project(z_a)

    for parity, (z_w, z_r) in enumerate(((z_a, z_b), (z_b, z_a))):
        @pl.when((t > 0) & (t < nt) & (t % 2 == parity))
        def _():
            project(z_w)
            gate(z_r)

    @pl.when(t == nt)
    def _():
        gate(z_a if nt % 2 == 1 else z_b)


def _sgu_branch(x, g, w_in, idx, ln_g, ln_b, w_s, b_s, *, tm):
    m, d = x.shape
    gdim = ln_g.shape[0]
    wblk = w_in.shape[2] // (2 * gdim) - 1
    nt = m // tm

    def proj(t):
        return jnp.minimum(t, nt - 1)

    def gated(t):
        return jnp.maximum(t - 1, 0)

    return pl.pallas_call(
        functools.partial(_sgu_branch_kernel, tm=tm, nt=nt),
        grid=(nt + 1,),
        in_specs=[
            pl.BlockSpec((tm, d), lambda t: (proj(t), 0)),
            pl.BlockSpec((1, d), lambda t: (0, 0)),
            pl.BlockSpec((None, d, 2 * gdim), lambda t: (idx, 0, wblk), pipeline_mode=pl.Buffered(1)),
            pl.BlockSpec((1, gdim), lambda t: (0, 0)),
            pl.BlockSpec((1, gdim), lambda t: (0, 0)),
            pl.BlockSpec((SGU_HEADS, CHUNK, CHUNK), lambda t: (0, 0, 0)),
            pl.BlockSpec((CHUNK, SGU_HEADS), lambda t: (0, 0)),
        ],
        out_specs=[pl.BlockSpec((tm, gdim), lambda t: (gated(t), 0)), pl.BlockSpec((tm, d), lambda t: (proj(t), 0))],
        out_shape=[jax.ShapeDtypeStruct((m, gdim), BF16), jax.ShapeDtypeStruct((m, d), BF16)],
        scratch_shapes=[pltpu.VMEM((d, 2 * gdim), BF16), pltpu.VMEM((tm, 2 * gdim), F32),
                        pltpu.VMEM((tm, 2 * gdim), F32)],
        compiler_params=_params("arbitrary"),
        name="sgu_branch",
    )(x, g.reshape(1, d), w_in, ln_g.reshape(1, gdim), ln_b.reshape(1, gdim), w_s, b_s.T)


def _conv_step(h_ref, wa_ref, wg_ref, dwb_ref, b_ref, lg_ref, lb_ref, o_ref, stage_w, stage_r, *, seq):
    cw = o_ref.shape[-1]
    gd = CONV_GROUP_DIM
    w = jnp.concatenate([wa_ref[...].astype(BF16), wg_ref[...].astype(BF16)], axis=1)

    def glu(c):
        z = jnp.dot(h_ref[c * ROW_CHUNK:(c + 1) * ROW_CHUNK, :], w, preferred_element_type=F32)
        start = CONV_HALO + c * ROW_CHUNK
        u = z[:, :cw] * jax.nn.sigmoid(z[:, cw:])
        for grp in range(cw // gd):
            stage_w[grp, start:start + ROW_CHUNK, :] = u[:, grp * gd:(grp + 1) * gd]

    def conv(c):
        for s in range(ROW_CHUNK // CONV_ROWS):
            r0 = c * ROW_CHUNK + s * CONV_ROWS
            base = r0 + CONV_HALO - CONV_PAD
            for grp in range(cw // gd):
                cols = slice(grp * gd, (grp + 1) * gd)
                acc = jnp.zeros((CONV_ROWS // SUBLANES, SUBLANES, gd), F32)
                for k in range(CONV_WIDTH):
                    tap = stage_r[grp, base + k:base + k + CONV_ROWS, :].reshape(CONV_ROWS // SUBLANES, SUBLANES, gd)
                    acc = acc + tap * dwb_ref[k * SUBLANES:(k + 1) * SUBLANES, cols]
                y = _layer_norm(acc.reshape(CONV_ROWS, gd) + b_ref[:, cols], lg_ref[:, cols], lb_ref[:, cols])
                o_ref[r0:r0 + CONV_ROWS, cols] = _silu(y).astype(o_ref.dtype)

    for c in range(seq // ROW_CHUNK):
        glu(c)
        conv(c)


def _conv_branch_kernel(h_ref, wa_ref, wg_ref, dw_ref, b_ref, lg_ref, lb_ref, o_ref, stage_a, stage_b, dwb_ref, *, seq):
    t = pl.program_id(0)
    cw = o_ref.shape[-1]

    @pl.when(t == 0)
    def _():
        stage_a[...] = jnp.zeros_like(stage_a)
        stage_b[...] = jnp.zeros_like(stage_b)

    for k in range(CONV_WIDTH):
        dwb_ref[k * SUBLANES:(k + 1) * SUBLANES, :] = jnp.broadcast_to(dw_ref[k:k + 1, :], (SUBLANES, cw))

    step = functools.partial(_conv_step, h_ref, wa_ref, wg_ref, dwb_ref, b_ref, lg_ref, lb_ref, o_ref, seq=seq)

    @pl.when(t % 2 == 0)
    def _():
        step(stage_a, stage_b)

    @pl.when(t % 2 == 1)
    def _():
        step(stage_b, stage_a)


def _conv_branch(h, w_in, idx, dw_w, dw_b, ln_g, ln_b):
    bsz, seq, d = h.shape
    cdim = dw_w.shape[1]
    cw = CONV_BLOCK
    nblk = cdim // cw
    items = bsz * nblk

    def proj(t):
        return jnp.minimum(t, items - 1)

    def conv(t):
        return jnp.maximum(t - 1, 0)

    vec = pl.BlockSpec((1, cw), lambda t: (0, conv(t) % nblk))
    stage = (cw // CONV_GROUP_DIM, seq + 2 * CONV_HALO, CONV_GROUP_DIM)
    return pl.pallas_call(
        functools.partial(_conv_branch_kernel, seq=seq),
        grid=(items + 1,),
        in_specs=[
            pl.BlockSpec((None, seq, d), lambda t: (proj(t) // nblk, 0, 0)),
            pl.BlockSpec((None, d, cw), lambda t: (idx, 0, proj(t) % nblk)),
            pl.BlockSpec((None, d, cw), lambda t: (idx, 0, nblk + proj(t) % nblk)),
            pl.BlockSpec((CONV_WIDTH, cw), lambda t: (0, conv(t) % nblk)),
            vec, vec, vec,
        ],
        out_specs=pl.BlockSpec((None, seq, cw), lambda t: (conv(t) // nblk, 0, conv(t) % nblk)),
        out_shape=jax.ShapeDtypeStruct((bsz, seq, cdim), BF16),
        scratch_shapes=[pltpu.VMEM(stage, F32), pltpu.VMEM(stage, F32), pltpu.VMEM((CONV_WIDTH * SUBLANES, cw), F32)],
        compiler_params=_params("arbitrary"),
        name="conv_branch",
    )(h, w_in, w_in, dw_w, dw_b.reshape(1, cdim), ln_g.reshape(1, cdim), ln_b.reshape(1, cdim))


def _proj_residual_kernel(*refs, n_lhs, has_bias):
    lhs_refs = refs[:n_lhs]
    w_ref, x_ref = refs[n_lhs], refs[n_lhs + 1]
    o_ref, wb_ref = refs[-2], refs[-1]

    @pl.when(pl.program_id(0) == 0)
    def _():
        _cast_weight(w_ref, wb_ref)

    acc = x_ref[...]
    if has_bias:
        acc = acc + refs[n_lhs + 2][...]
    k0 = 0
    for lhs_ref in lhs_refs:
        kw = lhs_ref.shape[-1]
        acc = acc + jnp.dot(lhs_ref[...], wb_ref[k0:k0 + kw, :], preferred_element_type=F32)
        k0 += kw
    o_ref[...] = acc


def _proj_residual(lhs_list, w, idx, x, bias=None, *, tm):
    m, n = x.shape
    k = w.shape[1]
    in_specs = [pl.BlockSpec((tm, lhs.shape[1]), lambda i: (i, 0)) for lhs in lhs_list]
    in_specs += [pl.BlockSpec((None, k, n), lambda i: (idx, 0, 0), pipeline_mode=pl.Buffered(1)),
                 pl.BlockSpec((tm, n), lambda i: (i, 0))]
    args = list(lhs_list) + [w, x]
    if bias is not None:
        in_specs.append(pl.BlockSpec((1, n), lambda i: (0, 0)))
        args.append(bias.reshape(1, n))
    return pl.pallas_call(
        functools.partial(_proj_residual_kernel, n_lhs=len(lhs_list), has_bias=bias is not None),
        grid=(m // tm,),
        in_specs=in_specs,
        out_specs=pl.BlockSpec((tm, n), lambda i: (i, 0)),
        out_shape=jax.ShapeDtypeStruct((m, n), F32),
        scratch_shapes=[pltpu.VMEM((k, n), BF16)],
        compiler_params=_params("arbitrary"),
        name="proj_residual",
    )(*args)


def _ffn_kernel(x_hbm, g_ref, wg_hbm, wu_hbm, wd_hbm, fg_ref, o_ref, h_ref, x_buf, wg_buf, wu_buf, wd_buf, x_sem, w_sem,
                *, layer, nf, final_norm):
    i = pl.program_id(0)
    ni = pl.num_programs(0)
    tm = x_buf.shape[0]
    tf = wg_buf.shape[-1]

    def x_copy(tile):
        return pltpu.make_async_copy(x_hbm.at[pl.ds(pl.multiple_of(tile * tm, tm), tm), :], x_buf, x_sem)

    def w_copies(f, slot):
        cols = pl.ds(pl.multiple_of(f * tf, tf), tf)
        return (pltpu.make_async_copy(wg_hbm.at[layer, :, cols], wg_buf.at[slot], w_sem.at[0, slot]),
                pltpu.make_async_copy(wu_hbm.at[layer, :, cols], wu_buf.at[slot], w_sem.at[1, slot]),
                pltpu.make_async_copy(wd_hbm.at[layer, cols, :], wd_buf.at[slot], w_sem.at[2, slot]))

    @pl.when(i == 0)
    def _():
        x_copy(0).start()
        for c in w_copies(0, 0):
            c.start()

    x_copy(i).wait()
    x = x_buf[...]
    h_ref[...] = _rms(x, g_ref[...]).astype(BF16)
    o_ref[...] = x

    @pl.when(i + 1 < ni)
    def _():
        x_copy(i + 1).start()

    def hidden_step(f, carry):
        n = i * nf + f
        slot = n % 2
        for c in w_copies(f, slot):
            c.wait()

        @pl.when(n + 1 < ni * nf)
        def _():
            for c in w_copies(jnp.where(f + 1 == nf, 0, f + 1), 1 - slot):
                c.start()

        w_gu = jnp.concatenate([wg_buf[slot].astype(BF16), wu_buf[slot].astype(BF16)], axis=1)
        gu = jnp.dot(h_ref[...], w_gu, preferred_element_type=F32)
        act = (_silu(gu[:, :tf]) * gu[:, tf:]).astype(BF16)
        o_ref[...] += jnp.dot(act, wd_buf[slot].astype(BF16), preferred_element_type=F32)
        return carry

    lax.fori_loop(0, nf, hidden_step, 0)

    if final_norm:
        o_ref[...] = _rms(o_ref[...], fg_ref[...])


def _ffn(x, g, w_gate, w_up, w_down, idx, final_g, *, tm, tf, final_norm):
    m, d = x.shape
    dff = w_gate.shape[2]
    return pl.pallas_call(
        functools.partial(_ffn_kernel, layer=idx, nf=dff // tf, final_norm=final_norm),
        grid=(m // tm,),
        in_specs=[
            pl.BlockSpec(memory_space=pl.ANY),
            pl.BlockSpec((1, d), lambda i: (0, 0)),
            pl.BlockSpec(memory_space=pl.ANY),
            pl.BlockSpec(memory_space=pl.ANY),
            pl.BlockSpec(memory_space=pl.ANY),
            pl.BlockSpec((1, d), lambda i: (0, 0)),
        ],
        out_specs=pl.BlockSpec((tm, d), lambda i: (i, 0)),
        out_shape=jax.ShapeDtypeStruct((m, d), F32),
        scratch_shapes=[pltpu.VMEM((tm, d), BF16), pltpu.VMEM((tm, d), F32),
                        pltpu.VMEM((2, d, tf), F32), pltpu.VMEM((2, d, tf), F32), pltpu.VMEM((2, tf, d), F32),
                        pltpu.SemaphoreType.DMA(()), pltpu.SemaphoreType.DMA((3, 2))],
        compiler_params=_params("arbitrary"),
        name="ffn",
    )(x, g.reshape(1, d), w_gate, w_up, w_down, final_g.reshape(1, d))


def _fnet_channel_kernel(x_ref, g_ref, cs_ref, a_ref, b_ref):
    h = _rms(x_ref[...], g_ref[...]).astype(BF16)
    gd = cs_ref.shape[0]
    for grp in range(FNET_GROUPS):
        cols = slice(grp * gd, (grp + 1) * gd)
        ab = jnp.dot(h[:, cols], cs_ref[...], preferred_element_type=F32)
        a_ref[:, cols] = ab[:, :gd].astype(a_ref.dtype)
        b_ref[:, cols] = ab[:, gd:].astype(b_ref.dtype)


def _fnet_channel(x, g, cs, *, tm):
    m, d = x.shape
    gd = cs.shape[0]
    out = jax.ShapeDtypeStruct((m, d), BF16)
    return pl.pallas_call(
        _fnet_channel_kernel,
        grid=(m // tm,),
        in_specs=[
            pl.BlockSpec((tm, d), lambda i: (i, 0)),
            pl.BlockSpec((1, d), lambda i: (0, 0)),
            pl.BlockSpec((gd, 2 * gd), lambda i: (0, 0)),
        ],
        out_specs=[pl.BlockSpec((tm, d), lambda i: (i, 0))] * 2,
        out_shape=[out, out],
        compiler_params=_params("parallel"),
        name="fnet_channel",
    )(x, g.reshape(1, d), cs)


def _fnet_seq_kernel(t_ref, p_ref, a_ref, b_ref, o_ref, uv_ref, y_ref):
    q = t_ref.shape[1]

    def fold(rows):
        a0, a1, a2, a3 = [a_ref[pl.ds(m * q + rows.start, rows.size), :].astype(F32) for m in range(FFT_RADIX)]
        b0, b1, b2, b3 = [b_ref[pl.ds(m * q + rows.start, rows.size), :].astype(F32) for m in range(FFT_RADIX)]
        ae, ao, ad, aq = a0 + a2, a1 + a3, a0 - a2, a1 - a3
        be, bo, bd, bq = b0 + b2, b1 + b3, b0 - b2, b1 - b3
        u = (ae + ao, ad - bq, ae - ao, ad + bq)
        v = (be + bo, bd + aq, be - bo, bd - aq)
        for r in range(FFT_RADIX):
            uv_ref[r, rows, :] = u[r].astype(BF16)
            uv_ref[r, pl.ds(q + rows.start, rows.size), :] = v[r].astype(BF16)

    _for_row_chunks(q, fold)
    for r in range(FFT_RADIX):
        y_ref[r] = jnp.dot(t_ref[r], uv_ref[r], preferred_element_type=F32).astype(BF16)
    rows = q // FFT_RADIX
    for kt in range(FFT_RADIX):
        blk = jnp.concatenate([y_ref[r, kt * rows:(kt + 1) * rows, :] for r in range(FFT_RADIX)], axis=0)
        o_ref[kt * q:(kt + 1) * q, :] = jnp.dot(p_ref[...], blk, preferred_element_type=F32).astype(o_ref.dtype)


def _fnet_seq(tables, perm, a, b, *, tn):
    bsz, seq, d = a.shape
    q = seq // FFT_RADIX
    return pl.pallas_call(
        _fnet_seq_kernel,
        grid=(bsz, d // tn),
        in_specs=[
            pl.BlockSpec((FFT_RADIX, q, 2 * q), lambda bb, j: (0, 0, 0)),
            pl.BlockSpec((q, q), lambda bb, j: (0, 0)),
            pl.BlockSpec((None, seq, tn), lambda bb, j: (bb, 0, j)),
            pl.BlockSpec((None, seq, tn), lambda bb, j: (bb, 0, j)),
        ],
        out_specs=pl.BlockSpec((None, seq, tn), lambda bb, j: (bb, 0, j)),
        out_shape=jax.ShapeDtypeStruct((bsz, seq, d), BF16),
        scratch_shapes=[pltpu.VMEM((FFT_RADIX, 2 * q, tn), BF16), pltpu.VMEM((FFT_RADIX, q, tn), BF16)],
        compiler_params=_params("parallel", "parallel"),
        name="fnet_seq",
    )(tables, perm, a, b)


def _interleave_matrix(q):
    dst = jnp.arange(q, dtype=jnp.int32)
    src = (dst % FFT_RADIX) * (q // FFT_RADIX) + dst // FFT_RADIX
    return (src[:, None] == jnp.arange(q, dtype=jnp.int32)[None, :]).astype(BF16)


def _cos_sin(rows, cols, n):
    ang = ((rows[:, None] * cols[None, :]) % n).astype(F32) * (2.0 * math.pi / n)
    return jnp.cos(ang), jnp.sin(ang)


def _dft_tables(n, scale):
    q = 1 << (int(math.log2(n)) // 2)
    cols = jnp.arange(n, dtype=jnp.int32)
    ca, sa = _cos_sin(jnp.arange(n // q, dtype=jnp.int32) * q, cols, n)
    cb, sb = _cos_sin(jnp.arange(q, dtype=jnp.int32), cols, n)
    ca, sa = (ca * scale)[:, None, :], (sa * scale)[:, None, :]
    cb, sb = cb[None, :, :], sb[None, :, :]
    return (ca * cb - sa * sb).reshape(n, n), (sa * cb + ca * sb).reshape(n, n)


def _radix_tables(n):
    q = n // FFT_RADIX
    c4, s4 = _dft_tables(q, 1.0 / math.sqrt(n))
    cr, sr = _cos_sin(jnp.arange(FFT_RADIX, dtype=jnp.int32), jnp.arange(q, dtype=jnp.int32), n)
    cr, sr = cr[:, None, :], sr[:, None, :]
    c = c4[None] * cr - s4[None] * sr
    sn = s4[None] * cr + c4[None] * sr
    return jnp.concatenate([c, -sn], axis=2)


def kernel(x, mix_norm_g, ffn_norm_g, final_norm_g, ab_w_in, conv_dw_w, conv_dw_b, conv_ln_g, conv_ln_b,
           sgu_ln_g, sgu_ln_b, sgu_w, sgu_b, ab_w_out, fnet_w_out, fnet_b_out, ffn_w_gate, ffn_w_up, ffn_w_down):
    bsz, seq, d = x.shape
    m = bsz * seq
    depth = mix_norm_g.shape[0]
    xf = x.reshape(m, d)
    for layer in range(depth):
        if layer % 2 == 0:
            i = layer // 2
            y_sgu, h = _sgu_branch(xf, mix_norm_g[layer], ab_w_in, i, sgu_ln_g[i], sgu_ln_b[i], sgu_w[i], sgu_b[i],
                                   tm=512)
            y_conv = _conv_branch(h.reshape(bsz, seq, d), ab_w_in, i, conv_dw_w[i], conv_dw_b[i], conv_ln_g[i],
                                  conv_ln_b[i])
            xf = _proj_residual([y_conv.reshape(m, -1), y_sgu], ab_w_out, i, xf, tm=512)
        else:
            j = layer // 2
            gd = d // FNET_GROUPS
            cc, sc = _dft_tables(gd, 1.0 / math.sqrt(gd))
            a, b = _fnet_channel(xf, mix_norm_g[layer], jnp.concatenate([cc, sc], axis=1).astype(BF16), tm=512)
            y = _fnet_seq(_radix_tables(seq).astype(BF16), _interleave_matrix(seq // FFT_RADIX),
                          a.reshape(bsz, seq, d), b.reshape(bsz, seq, d), tn=1024)
            xf = _proj_residual([y.reshape(m, d)], fnet_w_out, j, xf, fnet_b_out[j], tm=512)
        xf = _ffn(xf, ffn_norm_g[layer], ffn_w_gate, ffn_w_up, ffn_w_down, layer, final_norm_g,
                  tm=1024, tf=512, final_norm=layer == depth - 1)
    return xf.reshape(bsz, seq, d)
```

```python
import functools
import math

import jax
import jax.numpy as jnp
from jax import lax
from jax.experimental import pallas as pl
from jax.experimental.pallas import tpu as pltpu

F32 = jnp.float32
BF16 = jnp.bfloat16

RMS_EPS = 1e-6
LN_EPS = 1e-5
CONV_GROUP_DIM = 128
CONV_WIDTH = 31
CONV_PAD = (CONV_WIDTH - 1) // 2
CONV_HALO = 16
CONV_ROWS = 64
CONV_BLOCK = 256
SUBLANES = 8
SGU_HEADS = 8
CHUNK = 128
FNET_GROUPS = 8
FFT_RADIX = 4

V7X_VMEM_BYTES = 64 * 1024 * 1024
VMEM_LIMIT_BYTES = V7X_VMEM_BYTES - 3 * 1024 * 1024
ROW_CHUNK = 256


def _params(*semantics, flags=None):
    return pltpu.CompilerParams(dimension_semantics=semantics, vmem_limit_bytes=VMEM_LIMIT_BYTES, flags=flags)


def _rms(x, g):
    return x * lax.rsqrt(jnp.mean(x * x, axis=-1, keepdims=True) + RMS_EPS) * g


def _layer_norm(x, g, b):
    mu = jnp.mean(x, axis=-1, keepdims=True)
    xc = x - mu
    return xc * lax.rsqrt(jnp.mean(xc * xc, axis=-1, keepdims=True) + LN_EPS) * g + b


def _gelu(x):
    return 0.5 * x * (1.0 + lax.erf(x * (1.0 / math.sqrt(2.0))))


def _silu(x):
    return x * jax.nn.sigmoid(x)


def _for_row_chunks(nrows, fn):
    def body(r, carry):
        fn(pl.ds(pl.multiple_of(r * ROW_CHUNK, ROW_CHUNK), ROW_CHUNK))
        return carry
    lax.fori_loop(0, nrows // ROW_CHUNK, body, 0)


def _cast_weight(w_ref, wb_ref):
    def cast(rows):
        wb_ref[rows, :] = w_ref[rows, :].astype(BF16)
    _for_row_chunks(w_ref.shape[0], cast)


def _sgu_branch_kernel(x_ref, g_ref, w_ref, lg_ref, lb_ref, ws_ref, bs_ref, o_ref, h_ref, wb_ref, z_a, z_b, *, tm, nt):
    t = pl.program_id(0)
    gdim = o_ref.shape[-1]
    hd = gdim // SGU_HEADS

    def project(z_w):
        for r in range(tm // ROW_CHUNK):
            rows = slice(r * ROW_CHUNK, (r + 1) * ROW_CHUNK)
            h = _rms(x_ref[rows, :], g_ref[...]).astype(BF16)
            h_ref[rows, :] = h
            z_w[rows, :] = jnp.dot(h, wb_ref[...], preferred_element_type=F32)

    def gate(z_r):
        for c in range(tm // CHUNK):
            rows = slice(c * CHUNK, (c + 1) * CHUNK)
            v = _layer_norm(_gelu(z_r[rows, gdim:]), lg_ref[...], lb_ref[...]).astype(BF16)
            for hh in range(SGU_HEADS):
                cols = slice(hh * hd, (hh + 1) * hd)
                mixed = jnp.dot(ws_ref[hh].astype(BF16), v[:, cols], preferred_element_type=F32)
                mixed = mixed + bs_ref[:, hh:hh + 1]
                o_ref[rows, cols] = (_gelu(z_r[rows, cols]) * mixed).astype(o_ref.dtype)

    @pl.when(t == 0)
    def _():
        _cast_weight(w_ref, wb_ref)
        project(z_a)

    for parity, (z_w, z_r) in enumerate(((z_a, z_b), (z_b, z_a))):
        @pl.when((t > 0) & (t < nt) & (t % 2 == parity))
        def _():
            project(z_w)
            gate(z_r)

    @pl.when(t == nt)
    def _():
        gate(z_a if nt % 2 == 1 else z_b)


def _sgu_branch(x, g, w_in, idx, ln_g, ln_b, w_s, b_s, *, tm):
    m, d = x.shape
    gdim = ln_g.shape[0]
    wblk = w_in.shape[2] // (2 * gdim) - 1
    nt = m // tm

    def proj(t):
        return jnp.minimum(t, nt - 1)

    def gated(t):
        return jnp.maximum(t - 1, 0)

    return pl.pallas_call(
        functools.partial(_sgu_branch_kernel, tm=tm, nt=nt),
        grid=(nt + 1,),
        in_specs=[
            pl.BlockSpec((tm, d), lambda t: (proj(t), 0)),
            pl.BlockSpec((1, d), lambda t: (0, 0)),
            pl.BlockSpec((None, d, 2 * gdim), lambda t: (idx, 0, wblk), pipeline_mode=pl.Buffered(1)),
            pl.BlockSpec((1, gdim), lambda t: (0, 0)),
            pl.BlockSpec((1, gdim), lambda t: (0, 0)),
            pl.BlockSpec((SGU_HEADS, CHUNK, CHUNK), lambda t: (0, 0, 0)),
            pl.BlockSpec((CHUNK, SGU_HEADS), lambda t: (0, 0)),
        ],
        out_specs=[pl.BlockSpec((tm, gdim), lambda t: (gated(t), 0)), pl.BlockSpec((tm, d), lambda t: (proj(t), 0))],
        out_shape=[jax.ShapeDtypeStruct((m, gdim), BF16), jax.ShapeDtypeStruct((m, d), BF16)],
        scratch_shapes=[pltpu.VMEM((d, 2 * gdim), BF16), pltpu.VMEM((tm, 2 * gdim), F32),
                        pltpu.VMEM((tm, 2 * gdim), F32)],
        compiler_params=_params("arbitrary"),
        name="sgu_branch",
    )(x, g.reshape(1, d), w_in, ln_g.reshape(1, gdim), ln_b.reshape(1, gdim), w_s, b_s.T)


def _conv_step(h_ref, wa_ref, wg_ref, dwb_ref, b_ref, lg_ref, lb_ref, o_ref, stage_w, stage_r, *, seq):
    cw = o_ref.shape[-1]
    gd = CONV_GROUP_DIM
    w = jnp.concatenate([wa_ref[...].astype(BF16), wg_ref[...].astype(BF16)], axis=1)

    def glu(c):
        z = jnp.dot(h_ref[c * ROW_CHUNK:(c + 1) * ROW_CHUNK, :], w, preferred_element_type=F32)
        start = CONV_HALO + c * ROW_CHUNK
        u = z[:, :cw] * jax.nn.sigmoid(z[:, cw:])
        for grp in range(cw // gd):
            stage_w[grp, start:start + ROW_CHUNK, :] = u[:, grp * gd:(grp + 1) * gd]

    def conv(c):
        for s in range(ROW_CHUNK // CONV_ROWS):
            r0 = c * ROW_CHUNK + s * CONV_ROWS
            base = r0 + CONV_HALO - CONV_PAD
            for grp in range(cw // gd):
                cols = slice(grp * gd, (grp + 1) * gd)
                acc = jnp.zeros((CONV_ROWS // SUBLANES, SUBLANES, gd), F32)
                for k in range(CONV_WIDTH):
                    tap = stage_r[grp, base + k:base + k + CONV_ROWS, :].reshape(CONV_ROWS // SUBLANES, SUBLANES, gd)
                    acc = acc + tap * dwb_ref[k * SUBLANES:(k + 1) * SUBLANES, cols]
                y = _layer_norm(acc.reshape(CONV_ROWS, gd) + b_ref[:, cols], lg_ref[:, cols], lb_ref[:, cols])
                o_ref[r0:r0 + CONV_ROWS, cols] = _silu(y).astype(o_ref.dtype)

    for c in range(seq // ROW_CHUNK):
        glu(c)
        conv(c)


def _conv_branch_kernel(h_ref, wa_ref, wg_ref, dw_ref, b_ref, lg_ref, lb_ref, o_ref, stage_a, stage_b, dwb_ref, *, seq):
    t = pl.program_id(0)
    cw = o_ref.shape[-1]

    @pl.when(t == 0)
    def _():
        stage_a[...] = jnp.zeros_like(stage_a)
        stage_b[...] = jnp.zeros_like(stage_b)

    for k in range(CONV_WIDTH):
        dwb_ref[k * SUBLANES:(k + 1) * SUBLANES, :] = jnp.broadcast_to(dw_ref[k:k + 1, :], (SUBLANES, cw))

    step = functools.partial(_conv_step, h_ref, wa_ref, wg_ref, dwb_ref, b_ref, lg_ref, lb_ref, o_ref, seq=seq)

    @pl.when(t % 2 == 0)
    def _():
        step(stage_a, stage_b)

    @pl.when(t % 2 == 1)
    def _():
        step(stage_b, stage_a)


def _conv_branch(h, w_in, idx, dw_w, dw_b, ln_g, ln_b):
    bsz, seq, d = h.shape
    cdim = dw_w.shape[1]
    cw = CONV_BLOCK
    nblk = cdim // cw
    items = bsz * nblk

    def proj(t):
        return jnp.minimum(t, items - 1)

    def conv(t):
        return jnp.maximum(t - 1, 0)

    vec = pl.BlockSpec((1, cw), lambda t: (0, conv(t) % nblk))
    stage = (cw // CONV_GROUP_DIM, seq + 2 * CONV_HALO, CONV_GROUP_DIM)
    return pl.pallas_call(
        functools.partial(_conv_branch_kernel, seq=seq),
        grid=(items + 1,),
        in_specs=[
            pl.BlockSpec((None, seq, d), lambda t: (proj(t) // nblk, 0, 0)),
            pl.BlockSpec((None, d, cw), lambda t: (idx, 0, proj(t) % nblk)),
            pl.BlockSpec((None, d, cw), lambda t: (idx, 0, nblk + proj(t) % nblk)),
            pl.BlockSpec((CONV_WIDTH, cw), lambda t: (0, conv(t) % nblk)),
            vec, vec, vec,
        ],
        out_specs=pl.BlockSpec((None, seq, cw), lambda t: (conv(t) // nblk, 0, conv(t) % nblk)),
        out_shape=jax.ShapeDtypeStruct((bsz, seq, cdim), BF16),
        scratch_shapes=[pltpu.VMEM(stage, F32), pltpu.VMEM(stage, F32), pltpu.VMEM((CONV_WIDTH * SUBLANES, cw), F32)],
        compiler_params=_params("arbitrary"),
        name="conv_branch",
    )(h, w_in, w_in, dw_w, dw_b.reshape(1, cdim), ln_g.reshape(1, cdim), ln_b.reshape(1, cdim))


def _proj_residual_kernel(*refs, n_lhs, has_bias):
    lhs_refs = refs[:n_lhs]
    w_ref, x_ref = refs[n_lhs], refs[n_lhs + 1]
    o_ref, wb_ref = refs[-2], refs[-1]

    @pl.when(pl.program_id(0) == 0)
    def _():
        _cast_weight(w_ref, wb_ref)

    acc = x_ref[...]
    if has_bias:
        acc = acc + refs[n_lhs + 2][...]
    k0 = 0
    for lhs_ref in lhs_refs:
        kw = lhs_ref.shape[-1]
        acc = acc + jnp.dot(lhs_ref[...], wb_ref[k0:k0 + kw, :], preferred_element_type=F32)
        k0 += kw
    o_ref[...] = acc


def _proj_residual(lhs_list, w, idx, x, bias=None, *, tm):
    m, n = x.shape
    k = w.shape[1]
    in_specs = [pl.BlockSpec((tm, lhs.shape[1]), lambda i: (i, 0)) for lhs in lhs_list]
    in_specs += [pl.BlockSpec((None, k, n), lambda i: (idx, 0, 0), pipeline_mode=pl.Buffered(1)),
                 pl.BlockSpec((tm, n), lambda i: (i, 0))]
    args = list(lhs_list) + [w, x]
    if bias is not None:
        in_specs.append(pl.BlockSpec((1, n), lambda i: (0, 0)))
        args.append(bias.reshape(1, n))
    return pl.pallas_call(
        functools.partial(_proj_residual_kernel, n_lhs=len(lhs_list), has_bias=bias is not None),
        grid=(m // tm,),
        in_specs=in_specs,
        out_specs=pl.BlockSpec((tm, n), lambda i: (i, 0)),
        out_shape=jax.ShapeDtypeStruct((m, n), F32),
        scratch_shapes=[pltpu.VMEM((k, n), BF16)],
        compiler_params=_params("arbitrary"),
        name="proj_residual",
    )(*args)


def _ffn_kernel(x_hbm, g_ref, wg_ref, wu_ref, wd_ref, fg_ref, o_ref, h_ref, x_buf, x_sem, *, final_norm):
    i = pl.program_id(0)
    f = pl.program_id(1)
    tm = x_buf.shape[0]

    def x_copy(tile):
        return pltpu.make_async_copy(x_hbm.at[pl.ds(pl.multiple_of(tile * tm, tm), tm), :], x_buf, x_sem)

    @pl.when((i == 0) & (f == 0))
    def _():
        x_copy(0).start()

    @pl.when(f == 0)
    def _():
        x_copy(i).wait()
        x = x_buf[...]
        h_ref[...] = _rms(x, g_ref[...]).astype(BF16)
        o_ref[...] = x

    @pl.when((f == 1) & (i + 1 < pl.num_programs(0)))
    def _():
        x_copy(i + 1).start()

    tf = wg_ref.shape[-1]
    w_gu = jnp.concatenate([wg_ref[...].astype(BF16), wu_ref[...].astype(BF16)], axis=1)
    gu = jnp.dot(h_ref[...], w_gu, preferred_element_type=F32)
    act = (_silu(gu[:, :tf]) * gu[:, tf:]).astype(BF16)
    o_ref[...] += jnp.dot(act, wd_ref[...].astype(BF16), preferred_element_type=F32)

    if final_norm:
        @pl.when(f == pl.num_programs(1) - 1)
        def _():
            o_ref[...] = _rms(o_ref[...], fg_ref[...])


def _ffn(x, g, w_gate, w_up, w_down, idx, final_g, *, tm, tf, final_norm):
    m, d = x.shape
    dff = w_gate.shape[2]
    return pl.pallas_call(
        functools.partial(_ffn_kernel, final_norm=final_norm),
        grid=(m // tm, dff // tf),
        in_specs=[
            pl.BlockSpec(memory_space=pl.ANY),
            pl.BlockSpec((1, d), lambda i, f: (0, 0)),
            pl.BlockSpec((None, d, tf), lambda i, f: (idx, 0, f)),
            pl.BlockSpec((None, d, tf), lambda i, f: (idx, 0, f)),
            pl.BlockSpec((None, tf, d), lambda i, f: (idx, f, 0)),
            pl.BlockSpec((1, d), lambda i, f: (0, 0)),
        ],
        out_specs=pl.BlockSpec((tm, d), lambda i, f: (i, 0)),
        out_shape=jax.ShapeDtypeStruct((m, d), F32),
        scratch_shapes=[pltpu.VMEM((tm, d), BF16), pltpu.VMEM((tm, d), F32), pltpu.SemaphoreType.DMA(())],
        compiler_params=_params("arbitrary", "arbitrary"),
        name="ffn",
    )(x, g.reshape(1, d), w_gate, w_up, w_down, final_g.reshape(1, d))


def _fnet_channel_kernel(x_ref, g_ref, cs_ref, a_ref, b_ref):
    h = _rms(x_ref[...], g_ref[...]).astype(BF16)
    gd = cs_ref.shape[0]
    for grp in range(FNET_GROUPS):
        cols = slice(grp * gd, (grp + 1) * gd)
        ab = jnp.dot(h[:, cols], cs_ref[...], preferred_element_type=F32)
        a_ref[:, cols] = ab[:, :gd].astype(a_ref.dtype)
        b_ref[:, cols] = ab[:, gd:].astype(b_ref.dtype)


def _fnet_channel(x, g, cs, *, tm):
    m, d = x.shape
    gd = cs.shape[0]
    out = jax.ShapeDtypeStruct((m, d), BF16)
    return pl.pallas_call(
        _fnet_channel_kernel,
        grid=(m // tm,),
        in_specs=[
            pl.BlockSpec((tm, d), lambda i: (i, 0)),
            pl.BlockSpec((1, d), lambda i: (0, 0)),
            pl.BlockSpec((gd, 2 * gd), lambda i: (0, 0)),
        ],
        out_specs=[pl.BlockSpec((tm, d), lambda i: (i, 0))] * 2,
        out_shape=[out, out],
        compiler_params=_params("parallel"),
        name="fnet_channel",
    )(x, g.reshape(1, d), cs)


def _fnet_seq_kernel(t_ref, a_ref, b_ref, o_ref, uv_ref, y_ref):
    q = t_ref.shape[1]

    def fold(rows):
        a0, a1, a2, a3 = [a_ref[pl.ds(m * q + rows.start, rows.size), :].astype(F32) for m in range(FFT_RADIX)]
        b0, b1, b2, b3 = [b_ref[pl.ds(m * q + rows.start, rows.size), :].astype(F32) for m in range(FFT_RADIX)]
        ae, ao, ad, aq = a0 + a2, a1 + a3, a0 - a2, a1 - a3
        be, bo, bd, bq = b0 + b2, b1 + b3, b0 - b2, b1 - b3
        u = (ae + ao, ad - bq, ae - ao, ad + bq)
        v = (be + bo, bd + aq, be - bo, bd - aq)
        for r in range(FFT_RADIX):
            uv_ref[r, rows, :] = u[r].astype(BF16)
            uv_ref[r, pl.ds(q + rows.start, rows.size), :] = v[r].astype(BF16)

    _for_row_chunks(q, fold)
    for r in range(FFT_RADIX):
        y_ref[r] = jnp.dot(t_ref[r], uv_ref[r], preferred_element_type=F32).astype(BF16)
    rows = q // FFT_RADIX
    for kt in range(FFT_RADIX):
        blk = y_ref[:, kt * rows:(kt + 1) * rows, :]
        o_ref[kt * q:(kt + 1) * q, :] = jnp.swapaxes(blk, 0, 1).reshape(q, blk.shape[-1])


def _fnet_seq(tables, a, b, *, tn):
    bsz, seq, d = a.shape
    q = seq // FFT_RADIX
    return pl.pallas_call(
        _fnet_seq_kernel,
        grid=(bsz, d // tn),
        in_specs=[
            pl.BlockSpec((FFT_RADIX, q, 2 * q), lambda bb, j: (0, 0, 0)),
            pl.BlockSpec((None, seq, tn), lambda bb, j: (bb, 0, j)),
            pl.BlockSpec((None, seq, tn), lambda bb, j: (bb, 0, j)),
        ],
        out_specs=pl.BlockSpec((None, seq, tn), lambda bb, j: (bb, 0, j)),
        out_shape=jax.ShapeDtypeStruct((bsz, seq, d), BF16),
        scratch_shapes=[pltpu.VMEM((FFT_RADIX, 2 * q, tn), BF16), pltpu.VMEM((FFT_RADIX, q, tn), BF16)],
        compiler_params=_params("parallel", "parallel"),
        name="fnet_seq",
    )(tables, a, b)


def _cos_sin(rows, cols, n):
    ang = ((rows[:, None] * cols[None, :]) % n).astype(F32) * (2.0 * math.pi / n)
    return jnp.cos(ang), jnp.sin(ang)


def _dft_tables(n, scale):
    q = 1 << (int(math.log2(n)) // 2)
    cols = jnp.arange(n, dtype=jnp.int32)
    ca, sa = _cos_sin(jnp.arange(n // q, dtype=jnp.int32) * q, cols, n)
    cb, sb = _cos_sin(jnp.arange(q, dtype=jnp.int32), cols, n)
    ca, sa = (ca * scale)[:, None, :], (sa * scale)[:, None, :]
    cb, sb = cb[None, :, :], sb[None, :, :]
    return (ca * cb - sa * sb).reshape(n, n), (sa * cb + ca * sb).reshape(n, n)


def _radix_tables(n):
    q = n // FFT_RADIX
    c4, s4 = _dft_tables(q, 1.0 / math.sqrt(n))
    cr, sr = _cos_sin(jnp.arange(FFT_RADIX, dtype=jnp.int32), jnp.arange(q, dtype=jnp.int32), n)
    cr, sr = cr[:, None, :], sr[:, None, :]
    c = c4[None] * cr - s4[None] * sr
    sn = s4[None] * cr + c4[None] * sr
    return jnp.concatenate([c, -sn], axis=2)


def kernel(x, mix_norm_g, ffn_norm_g, final_norm_g, ab_w_in, conv_dw_w, conv_dw_b, conv_ln_g, conv_ln_b,
           sgu_ln_g, sgu_ln_b, sgu_w, sgu_b, ab_w_out, fnet_w_out, fnet_b_out, ffn_w_gate, ffn_w_up, ffn_w_down):
    bsz, seq, d = x.shape
    m = bsz * seq
    depth = mix_norm_g.shape[0]
    xf = x.reshape(m, d)
    for layer in range(depth):
        if layer % 2 == 0:
            i = layer // 2
            y_sgu, h = _sgu_branch(xf, mix_norm_g[layer], ab_w_in, i, sgu_ln_g[i], sgu_ln_b[i], sgu_w[i], sgu_b[i],
                                   tm=512)
            y_conv = _conv_branch(h.reshape(bsz, seq, d), ab_w_in, i, conv_dw_w[i], conv_dw_b[i], conv_ln_g[i],
                                  conv_ln_b[i])
            xf = _proj_residual([y_conv.reshape(m, -1), y_sgu], ab_w_out, i, xf, tm=512)
        else:
            j = layer // 2
            gd = d // FNET_GROUPS
            cc, sc = _dft_tables(gd, 1.0 / math.sqrt(gd))
            a, b = _fnet_channel(xf, mix_norm_g[layer], jnp.concatenate([cc, sc], axis=1).astype(BF16), tm=1024)
            y = _fnet_seq(_radix_tables(seq).astype(BF16), a.reshape(bsz, seq, d), b.reshape(bsz, seq, d), tn=1024)
            xf = _proj_residual([y.reshape(m, d)], fnet_w_out, j, xf, fnet_b_out[j], tm=512)
        xf = _ffn(xf, ffn_norm_g[layer], ffn_w_gate, ffn_w_up, ffn_w_down, layer, final_norm_g,
                  tm=1024, tf=512, final_norm=layer == depth - 1)
    return xf.reshape(bsz, seq, d)
```

```python
import functools
import math

import jax
import jax.numpy as jnp
from jax import lax
from jax.experimental import pallas as pl
from jax.experimental.pallas import tpu as pltpu

F32 = jnp.float32
BF16 = jnp.bfloat16

RMS_EPS = 1e-6
LN_EPS = 1e-5
CONV_GROUP_DIM = 128
CONV_WIDTH = 31
CONV_PAD = (CONV_WIDTH - 1) // 2
CONV_HALO = 16
CONV_ROWS = 64
CONV_BLOCK = 256
SUBLANES = 8
LANES = 128
SGU_HEADS = 8
CHUNK = 128
FNET_GROUPS = 8
FFT_RADIX = 4

V7X_VMEM_BYTES = 64 * 1024 * 1024
VMEM_LIMIT_BYTES = V7X_VMEM_BYTES - 3 * 1024 * 1024
ROW_CHUNK = 256


def _params(*semantics, flags=None):
    return pltpu.CompilerParams(dimension_semantics=semantics, vmem_limit_bytes=VMEM_LIMIT_BYTES, flags=flags)


def _rms(x, g):
    return x * lax.rsqrt(jnp.mean(x * x, axis=-1, keepdims=True) + RMS_EPS) * g


def _layer_norm(x, g, b):
    mu = jnp.mean(x, axis=-1, keepdims=True)
    xc = x - mu
    return xc * lax.rsqrt(jnp.mean(xc * xc, axis=-1, keepdims=True) + LN_EPS) * g + b


def _gelu(x):
    return 0.5 * x * (1.0 + lax.erf(x * (1.0 / math.sqrt(2.0))))


def _silu(x):
    return x * jax.nn.sigmoid(x)


def _for_row_chunks(nrows, fn):
    def body(r, carry):
        fn(pl.ds(pl.multiple_of(r * ROW_CHUNK, ROW_CHUNK), ROW_CHUNK))
        return carry
    lax.fori_loop(0, nrows // ROW_CHUNK, body, 0)


def _cast_weight(w_ref, wb_ref):
    def cast(rows):
        wb_ref[rows, :] = w_ref[rows, :].astype(BF16)
    _for_row_chunks(w_ref.shape[0], cast)


def _sgu_branch_kernel(x_ref, g_ref, w_ref, lg_ref, lb_ref, ws_ref, bs_ref, o_ref, h_ref, wb_ref, z_a, z_b, *, tm, nt):
    t = pl.program_id(0)
    gdim = o_ref.shape[-1]
    hd = gdim // SGU_HEADS

    def project(z_w):
        for r in range(tm // ROW_CHUNK):
            rows = slice(r * ROW_CHUNK, (r + 1) * ROW_CHUNK)
            h = _rms(x_ref[rows, :], g_ref[...]).astype(BF16)
            h_ref[rows, :] = h
            z_w[rows, :] = jnp.dot(h, wb_ref[...], preferred_element_type=F32)

    def gate(z_r):
        for c in range(tm // CHUNK):
            rows = slice(c * CHUNK, (c + 1) * CHUNK)
            v = _layer_norm(_gelu(z_r[rows, gdim:]), lg_ref[...], lb_ref[...]).astype(BF16)
            for hh in range(SGU_HEADS):
                cols = slice(hh * hd, (hh + 1) * hd)
                mixed = jnp.dot(ws_ref[hh].astype(BF16), v[:, cols], preferred_element_type=F32)
                mixed = mixed + bs_ref[:, hh:hh + 1]
                o_ref[rows, cols] = (_gelu(z_r[rows, cols]) * mixed).astype(o_ref.dtype)

    @pl.when(t == 0)
    def _():
        _cast_weight(w_ref, wb_ref)
        project(z_a)

    for parity, (z_w, z_r) in enumerate(((z_a, z_b), (z_b, z_a))):
        @pl.when((t > 0) & (t < nt) & (t % 2 == parity))
        def _():
            project(z_w)
            gate(z_r)

    @pl.when(t == nt)
    def _():
        gate(z_a if nt % 2 == 1 else z_b)


def _sgu_branch(x, g, w_in, idx, ln_g, ln_b, w_s, b_s, *, tm):
    m, d = x.shape
    gdim = ln_g.shape[0]
    wblk = w_in.shape[2] // (2 * gdim) - 1
    nt = m // tm

    def proj(t):
        return jnp.minimum(t, nt - 1)

    def gated(t):
        return jnp.maximum(t - 1, 0)

    return pl.pallas_call(
        functools.partial(_sgu_branch_kernel, tm=tm, nt=nt),
        grid=(nt + 1,),
        in_specs=[
            pl.BlockSpec((tm, d), lambda t: (proj(t), 0)),
            pl.BlockSpec((1, d), lambda t: (0, 0)),
            pl.BlockSpec((None, d, 2 * gdim), lambda t: (idx, 0, wblk), pipeline_mode=pl.Buffered(1)),
            pl.BlockSpec((1, gdim), lambda t: (0, 0)),
            pl.BlockSpec((1, gdim), lambda t: (0, 0)),
            pl.BlockSpec((SGU_HEADS, CHUNK, CHUNK), lambda t: (0, 0, 0)),
            pl.BlockSpec((CHUNK, SGU_HEADS), lambda t: (0, 0)),
        ],
        out_specs=[pl.BlockSpec((tm, gdim), lambda t: (gated(t), 0)), pl.BlockSpec((tm, d), lambda t: (proj(t), 0))],
        out_shape=[jax.ShapeDtypeStruct((m, gdim), BF16), jax.ShapeDtypeStruct((m, d), BF16)],
        scratch_shapes=[pltpu.VMEM((d, 2 * gdim), BF16), pltpu.VMEM((tm, 2 * gdim), F32),
                        pltpu.VMEM((tm, 2 * gdim), F32)],
        compiler_params=_params("arbitrary"),
        name="sgu_branch",
    )(x, g.reshape(1, d), w_in, ln_g.reshape(1, gdim), ln_b.reshape(1, gdim), w_s, b_s.T)


def _conv_step(h_ref, wa_ref, wg_ref, dwb_ref, b_ref, lg_ref, lb_ref, o_ref, stage_w, stage_r, *, seq):
    cw = o_ref.shape[-1]
    gd = CONV_GROUP_DIM
    w = jnp.concatenate([wa_ref[...].astype(BF16), wg_ref[...].astype(BF16)], axis=1)

    def glu(c):
        z = jnp.dot(h_ref[c * ROW_CHUNK:(c + 1) * ROW_CHUNK, :], w, preferred_element_type=F32)
        start = CONV_HALO + c * ROW_CHUNK
        u = z[:, :cw] * jax.nn.sigmoid(z[:, cw:])
        for grp in range(cw // gd):
            stage_w[grp, start:start + ROW_CHUNK, :] = u[:, grp * gd:(grp + 1) * gd]

    def conv(c):
        for s in range(ROW_CHUNK // CONV_ROWS):
            r0 = c * ROW_CHUNK + s * CONV_ROWS
            base = r0 + CONV_HALO - CONV_PAD
            for grp in range(cw // gd):
                cols = slice(grp * gd, (grp + 1) * gd)
                acc = jnp.zeros((CONV_ROWS // SUBLANES, SUBLANES, gd), F32)
                for k in range(CONV_WIDTH):
                    tap = stage_r[grp, base + k:base + k + CONV_ROWS, :].reshape(CONV_ROWS // SUBLANES, SUBLANES, gd)
                    acc = acc + tap * dwb_ref[k * SUBLANES:(k + 1) * SUBLANES, cols]
                y = _layer_norm(acc.reshape(CONV_ROWS, gd) + b_ref[:, cols], lg_ref[:, cols], lb_ref[:, cols])
                o_ref[r0:r0 + CONV_ROWS, cols] = _silu(y).astype(o_ref.dtype)

    for c in range(seq // ROW_CHUNK):
        glu(c)
        conv(c)


def _conv_branch_kernel(h_ref, wa_ref, wg_ref, dw_ref, b_ref, lg_ref, lb_ref, o_ref, stage_a, stage_b, dwb_ref, *, seq):
    t = pl.program_id(0)
    cw = o_ref.shape[-1]

    @pl.when(t == 0)
    def _():
        stage_a[...] = jnp.zeros_like(stage_a)
        stage_b[...] = jnp.zeros_like(stage_b)

    for k in range(CONV_WIDTH):
        dwb_ref[k * SUBLANES:(k + 1) * SUBLANES, :] = jnp.broadcast_to(dw_ref[k:k + 1, :], (SUBLANES, cw))

    step = functools.partial(_conv_step, h_ref, wa_ref, wg_ref, dwb_ref, b_ref, lg_ref, lb_ref, o_ref, seq=seq)

    @pl.when(t % 2 == 0)
    def _():
        step(stage_a, stage_b)

    @pl.when(t % 2 == 1)
    def _():
        step(stage_b, stage_a)


def _conv_branch(h, w_in, idx, dw_w, dw_b, ln_g, ln_b):
    bsz, seq, d = h.shape
    cdim = dw_w.shape[1]
    cw = CONV_BLOCK
    nblk = cdim // cw
    items = bsz * nblk

    def proj(t):
        return jnp.minimum(t, items - 1)

    def conv(t):
        return jnp.maximum(t - 1, 0)

    vec = pl.BlockSpec((1, cw), lambda t: (0, conv(t) % nblk))
    stage = (cw // CONV_GROUP_DIM, seq + 2 * CONV_HALO, CONV_GROUP_DIM)
    return pl.pallas_call(
        functools.partial(_conv_branch_kernel, seq=seq),
        grid=(items + 1,),
        in_specs=[
            pl.BlockSpec((None, seq, d), lambda t: (proj(t) // nblk, 0, 0)),
            pl.BlockSpec((None, d, cw), lambda t: (idx, 0, proj(t) % nblk)),
            pl.BlockSpec((None, d, cw), lambda t: (idx, 0, nblk + proj(t) % nblk)),
            pl.BlockSpec((CONV_WIDTH, cw), lambda t: (0, conv(t) % nblk)),
            vec, vec, vec,
        ],
        out_specs=pl.BlockSpec((None, seq, cw), lambda t: (conv(t) // nblk, 0, conv(t) % nblk)),
        out_shape=jax.ShapeDtypeStruct((bsz, seq, cdim), BF16),
        scratch_shapes=[pltpu.VMEM(stage, F32), pltpu.VMEM(stage, F32), pltpu.VMEM((CONV_WIDTH * SUBLANES, cw), F32)],
        compiler_params=_params("arbitrary"),
        name="conv_branch",
    )(h, w_in, w_in, dw_w, dw_b.reshape(1, cdim), ln_g.reshape(1, cdim), ln_b.reshape(1, cdim))


def _proj_residual_kernel(*refs, n_lhs, has_bias):
    lhs_refs = refs[:n_lhs]
    w_ref, x_ref = refs[n_lhs], refs[n_lhs + 1]
    o_ref, wb_ref = refs[-2], refs[-1]

    @pl.when(pl.program_id(0) == 0)
    def _():
        _cast_weight(w_ref, wb_ref)

    acc = x_ref[...]
    if has_bias:
        acc = acc + refs[n_lhs + 2][...]
    k0 = 0
    for lhs_ref in lhs_refs:
        kw = lhs_ref.shape[-1]
        acc = acc + jnp.dot(lhs_ref[...], wb_ref[k0:k0 + kw, :], preferred_element_type=F32)
        k0 += kw
    o_ref[...] = acc


def _proj_residual(lhs_list, w, idx, x, bias=None, *, tm):
    m, n = x.shape
    k = w.shape[1]
    in_specs = [pl.BlockSpec((tm, lhs.shape[1]), lambda i: (i, 0)) for lhs in lhs_list]
    in_specs += [pl.BlockSpec((None, k, n), lambda i: (idx, 0, 0), pipeline_mode=pl.Buffered(1)),
                 pl.BlockSpec((tm, n), lambda i: (i, 0))]
    args = list(lhs_list) + [w, x]
    if bias is not None:
        in_specs.append(pl.BlockSpec((1, n), lambda i: (0, 0)))
        args.append(bias.reshape(1, n))
    return pl.pallas_call(
        functools.partial(_proj_residual_kernel, n_lhs=len(lhs_list), has_bias=bias is not None),
        grid=(m // tm,),
        in_specs=in_specs,
        out_specs=pl.BlockSpec((tm, n), lambda i: (i, 0)),
        out_shape=jax.ShapeDtypeStruct((m, n), F32),
        scratch_shapes=[pltpu.VMEM((k, n), BF16)],
        compiler_params=_params("arbitrary"),
        name="proj_residual",
    )(*args)


def _ffn_kernel(x_hbm, g_ref, wg_ref, wu_ref, wd_ref, fg_ref, o_ref, *rest, final_norm):
    if final_norm:
        h_ref, x_buf, x_sem = rest
    else:
        rstd_ref, h_ref, x_buf, x_sem = rest
    i = pl.program_id(0)
    f = pl.program_id(1)
    tm = x_buf.shape[0]

    def x_copy(tile):
        return pltpu.make_async_copy(x_hbm.at[pl.ds(pl.multiple_of(tile * tm, tm), tm), :], x_buf, x_sem)

    @pl.when((i == 0) & (f == 0))
    def _():
        x_copy(0).start()

    @pl.when(f == 0)
    def _():
        x_copy(i).wait()
        x = x_buf[...]
        h_ref[...] = _rms(x, g_ref[...]).astype(BF16)
        o_ref[...] = x

    @pl.when((f == 1) & (i + 1 < pl.num_programs(0)))
    def _():
        x_copy(i + 1).start()

    tf = wg_ref.shape[-1]
    w_gu = jnp.concatenate([wg_ref[...].astype(BF16), wu_ref[...].astype(BF16)], axis=1)
    gu = jnp.dot(h_ref[...], w_gu, preferred_element_type=F32)
    act = (_silu(gu[:, :tf]) * gu[:, tf:]).astype(BF16)
    o_ref[...] += jnp.dot(act, wd_ref[...].astype(BF16), preferred_element_type=F32)

    @pl.when(f == pl.num_programs(1) - 1)
    def _():
        y = o_ref[...]
        if final_norm:
            o_ref[...] = _rms(y, fg_ref[...])
        else:
            rstd = lax.rsqrt(jnp.mean(y * y, axis=-1, keepdims=True) + RMS_EPS)
            rstd_ref[...] = jnp.broadcast_to(rstd, rstd_ref.shape)


def _ffn(x, g, w_gate, w_up, w_down, idx, final_g, *, tm, tf, final_norm):
    m, d = x.shape
    dff = w_gate.shape[2]
    out_specs = [pl.BlockSpec((tm, d), lambda i, f: (i, 0))]
    out_shape = [jax.ShapeDtypeStruct((m, d), F32)]
    if not final_norm:
        out_specs.append(pl.BlockSpec((tm, LANES), lambda i, f: (i, 0)))
        out_shape.append(jax.ShapeDtypeStruct((m, LANES), F32))
    return pl.pallas_call(
        functools.partial(_ffn_kernel, final_norm=final_norm),
        grid=(m // tm, dff // tf),
        in_specs=[
            pl.BlockSpec(memory_space=pl.ANY),
            pl.BlockSpec((1, d), lambda i, f: (0, 0)),
            pl.BlockSpec((None, d, tf), lambda i, f: (idx, 0, f)),
            pl.BlockSpec((None, d, tf), lambda i, f: (idx, 0, f)),
            pl.BlockSpec((None, tf, d), lambda i, f: (idx, f, 0)),
            pl.BlockSpec((1, d), lambda i, f: (0, 0)),
        ],
        out_specs=out_specs,
        out_shape=out_shape,
        scratch_shapes=[pltpu.VMEM((tm, d), BF16), pltpu.VMEM((tm, d), F32), pltpu.SemaphoreType.DMA(())],
        compiler_params=_params("arbitrary", "arbitrary"),
        name="ffn",
    )(x, g.reshape(1, d), w_gate, w_up, w_down, final_g.reshape(1, d))


def _fnet_mix_kernel(t_ref, cs_ref, x_ref, r_ref, g_ref, o_ref, p_ref, uv_ref, y_ref):
    q = t_ref.shape[1]
    tn = o_ref.shape[-1]
    gd = cs_ref.shape[0]

    def fold(rows):
        hq = []
        for m in range(FFT_RADIX):
            rs = pl.ds(m * q + rows.start, rows.size)
            rstd = jnp.concatenate([r_ref[rs, :]] * (tn // LANES), axis=1)
            hq.append(x_ref[rs, :] * rstd * g_ref[...])
        e, o, d, qq = hq[0] + hq[2], hq[1] + hq[3], hq[0] - hq[2], hq[1] - hq[3]
        for idx, val in enumerate((e + o, e - o, d, qq)):
            p_ref[idx, rows, :] = val.astype(BF16)

    _for_row_chunks(q, fold)

    for grp in range(tn // gd):
        cols = slice(grp * gd, (grp + 1) * gd)
        res = [jnp.dot(p_ref[idx, :, cols], cs_ref[...], preferred_element_type=F32) for idx in range(FFT_RADIX)]
        c = [r[:, :gd] for r in res]
        s = [r[:, gd:] for r in res]
        u = (c[0], c[2] - s[3], c[1], c[2] + s[3])
        v = (s[0], s[2] + c[3], s[1], s[2] - c[3])
        for r in range(FFT_RADIX):
            uv_ref[r, 0:q, cols] = u[r].astype(BF16)
            uv_ref[r, q:2 * q, cols] = v[r].astype(BF16)

    for r in range(FFT_RADIX):
        y_ref[r] = jnp.dot(t_ref[r], uv_ref[r], preferred_element_type=F32).astype(BF16)
    rows = q // FFT_RADIX
    for kt in range(FFT_RADIX):
        blk = y_ref[:, kt * rows:(kt + 1) * rows, :]
        o_ref[kt * q:(kt + 1) * q, :] = jnp.swapaxes(blk, 0, 1).reshape(q, blk.shape[-1])


def _fnet_mix(tables, cs, x, rstd, g, *, tn):
    bsz, seq, d = x.shape
    q = seq // FFT_RADIX
    gd = cs.shape[0]
    return pl.pallas_call(
        _fnet_mix_kernel,
        grid=(bsz, d // tn),
        in_specs=[
            pl.BlockSpec((FFT_RADIX, q, 2 * q), lambda bb, j: (0, 0, 0), pipeline_mode=pl.Buffered(1)),
            pl.BlockSpec((gd, 2 * gd), lambda bb, j: (0, 0)),
            pl.BlockSpec((None, seq, tn), lambda bb, j: (bb, 0, j)),
            pl.BlockSpec((None, seq, LANES), lambda bb, j: (bb, 0, 0)),
            pl.BlockSpec((1, tn), lambda bb, j: (0, j)),
        ],
        out_specs=pl.BlockSpec((None, seq, tn), lambda bb, j: (bb, 0, j)),
        out_shape=jax.ShapeDtypeStruct((bsz, seq, d), BF16),
        scratch_shapes=[pltpu.VMEM((FFT_RADIX, q, tn), BF16), pltpu.VMEM((FFT_RADIX, 2 * q, tn), BF16),
                        pltpu.VMEM((FFT_RADIX, q, tn), BF16)],
        compiler_params=_params("parallel", "parallel"),
        name="fnet_mix",
    )(tables, cs, x, rstd, g.reshape(1, d))


def _cos_sin(rows, cols, n):
    ang = ((rows[:, None] * cols[None, :]) % n).astype(F32) * (2.0 * math.pi / n)
    return jnp.cos(ang), jnp.sin(ang)


def _dft_tables(n, scale):
    q = 1 << (int(math.log2(n)) // 2)
    cols = jnp.arange(n, dtype=jnp.int32)
    ca, sa = _cos_sin(jnp.arange(n // q, dtype=jnp.int32) * q, cols, n)
    cb, sb = _cos_sin(jnp.arange(q, dtype=jnp.int32), cols, n)
    ca, sa = (ca * scale)[:, None, :], (sa * scale)[:, None, :]
    cb, sb = cb[None, :, :], sb[None, :, :]
    return (ca * cb - sa * sb).reshape(n, n), (sa * cb + ca * sb).reshape(n, n)


def _radix_tables(n):
    q = n // FFT_RADIX
    c4, s4 = _dft_tables(q, 1.0 / math.sqrt(n))
    cr, sr = _cos_sin(jnp.arange(FFT_RADIX, dtype=jnp.int32), jnp.arange(q, dtype=jnp.int32), n)
    cr, sr = cr[:, None, :], sr[:, None, :]
    c = c4[None] * cr - s4[None] * sr
    sn = s4[None] * cr + c4[None] * sr
    return jnp.concatenate([c, -sn], axis=2)


def kernel(x, mix_norm_g, ffn_norm_g, final_norm_g, ab_w_in, conv_dw_w, conv_dw_b, conv_ln_g, conv_ln_b,
           sgu_ln_g, sgu_ln_b, sgu_w, sgu_b, ab_w_out, fnet_w_out, fnet_b_out, ffn_w_gate, ffn_w_up, ffn_w_down):
    bsz, seq, d = x.shape
    m = bsz * seq
    depth = mix_norm_g.shape[0]
    xf = x.reshape(m, d)
    for layer in range(depth):
        if layer % 2 == 0:
            i = layer // 2
            y_sgu, h = _sgu_branch(xf, mix_norm_g[layer], ab_w_in, i, sgu_ln_g[i], sgu_ln_b[i], sgu_w[i], sgu_b[i],
                                   tm=512)
            y_conv = _conv_branch(h.reshape(bsz, seq, d), ab_w_in, i, conv_dw_w[i], conv_dw_b[i], conv_ln_g[i],
                                  conv_ln_b[i])
            xf = _proj_residual([y_conv.reshape(m, -1), y_sgu], ab_w_out, i, xf, tm=512)
        else:
            j = layer // 2
            gd = d // FNET_GROUPS
            cc, sc = _dft_tables(gd, 1.0 / math.sqrt(gd))
            y = _fnet_mix(_radix_tables(seq).astype(BF16), jnp.concatenate([cc, sc], axis=1).astype(BF16),
                          xf.reshape(bsz, seq, d), rstd.reshape(bsz, seq, LANES), mix_norm_g[layer], tn=1024)
            xf = _proj_residual([y.reshape(m, d)], fnet_w_out, j, xf, fnet_b_out[j], tm=512)
        outs = _ffn(xf, ffn_norm_g[layer], ffn_w_gate, ffn_w_up, ffn_w_down, layer, final_norm_g,
                    tm=1024, tf=512, final_norm=layer == depth - 1)
        xf = outs[0]
        rstd = outs[1] if len(outs) > 1 else None
    return xf.reshape(bsz, seq, d)
```

```python
import functools
import math

import jax
import jax.numpy as jnp
from jax import lax
from jax.experimental import pallas as pl
from jax.experimental.pallas import tpu as pltpu

F32 = jnp.float32
BF16 = jnp.bfloat16

RMS_EPS = 1e-6
LN_EPS = 1e-5
CONV_GROUP_DIM = 128
CONV_WIDTH = 31
CONV_PAD = (CONV_WIDTH - 1) // 2
CONV_HALO = 16
CONV_ROWS = 64
CONV_BLOCK = 256
SUBLANES = 8
LANES = 128
SGU_HEADS = 8
CHUNK = 128
FNET_GROUPS = 8
FFT_RADIX = 4

V7X_VMEM_BYTES = 64 * 1024 * 1024
VMEM_LIMIT_BYTES = V7X_VMEM_BYTES - 3 * 1024 * 1024
ROW_CHUNK = 256

SGU_ROWS = 512
PROJ_ROWS = 512
FFN_ROWS = 1024
FFN_HIDDEN = 512
FNET_COLS = 1024


def _params(*semantics):
    return pltpu.CompilerParams(dimension_semantics=semantics, vmem_limit_bytes=VMEM_LIMIT_BYTES)


def _rms(x, g):
    return x * lax.rsqrt(jnp.mean(x * x, axis=-1, keepdims=True) + RMS_EPS) * g


def _layer_norm(x, g, b):
    mu = jnp.mean(x, axis=-1, keepdims=True)
    xc = x - mu
    return xc * lax.rsqrt(jnp.mean(xc * xc, axis=-1, keepdims=True) + LN_EPS) * g + b


def _gelu(x):
    return 0.5 * x * (1.0 + lax.erf(x * (1.0 / math.sqrt(2.0))))


def _silu(x):
    return x * jax.nn.sigmoid(x)


def _for_row_chunks(nrows, fn):
    def body(r, carry):
        fn(pl.ds(pl.multiple_of(r * ROW_CHUNK, ROW_CHUNK), ROW_CHUNK))
        return carry
    lax.fori_loop(0, nrows // ROW_CHUNK, body, 0)


def _cast_weight(w_ref, wb_ref):
    def cast(rows):
        wb_ref[rows, :] = w_ref[rows, :].astype(BF16)
    _for_row_chunks(w_ref.shape[0], cast)


def _sgu_branch_kernel(x_ref, g_ref, w_ref, lg_ref, lb_ref, ws_ref, bs_ref, o_ref, h_ref, wb_ref, z_a, z_b, *, tm, nt):
    t = pl.program_id(0)
    gdim = o_ref.shape[-1]
    hd = gdim // SGU_HEADS

    def project(z_w):
        for r in range(tm // ROW_CHUNK):
            rows = slice(r * ROW_CHUNK, (r + 1) * ROW_CHUNK)
            h = _rms(x_ref[rows, :], g_ref[...]).astype(BF16)
            h_ref[rows, :] = h
            z_w[rows, :] = jnp.dot(h, wb_ref[...], preferred_element_type=F32)

    def gate(z_r):
        for c in range(tm // CHUNK):
            rows = slice(c * CHUNK, (c + 1) * CHUNK)
            v = _layer_norm(_gelu(z_r[rows, gdim:]), lg_ref[...], lb_ref[...]).astype(BF16)
            for hh in range(SGU_HEADS):
                cols = slice(hh * hd, (hh + 1) * hd)
                mixed = jnp.dot(ws_ref[hh].astype(BF16), v[:, cols], preferred_element_type=F32)
                mixed = mixed + bs_ref[:, hh:hh + 1]
                o_ref[rows, cols] = (_gelu(z_r[rows, cols]) * mixed).astype(o_ref.dtype)

    @pl.when(t == 0)
    def _():
        _cast_weight(w_ref, wb_ref)
        project(z_a)

    for parity, (z_w, z_r) in enumerate(((z_a, z_b), (z_b, z_a))):
        @pl.when((t > 0) & (t < nt) & (t % 2 == parity))
        def _():
            project(z_w)
            gate(z_r)

    @pl.when(t == nt)
    def _():
        gate(z_a if nt % 2 == 1 else z_b)


def _sgu_branch(x, g, w_in, idx, ln_g, ln_b, w_s, b_s, *, tm):
    m, d = x.shape
    gdim = ln_g.shape[0]
    wblk = w_in.shape[2] // (2 * gdim) - 1
    nt = m // tm

    def proj(t):
        return jnp.minimum(t, nt - 1)

    def gated(t):
        return jnp.maximum(t - 1, 0)

    return pl.pallas_call(
        functools.partial(_sgu_branch_kernel, tm=tm, nt=nt),
        grid=(nt + 1,),
        in_specs=[
            pl.BlockSpec((tm, d), lambda t: (proj(t), 0)),
            pl.BlockSpec((1, d), lambda t: (0, 0)),
            pl.BlockSpec((None, d, 2 * gdim), lambda t: (idx, 0, wblk), pipeline_mode=pl.Buffered(1)),
            pl.BlockSpec((1, gdim), lambda t: (0, 0)),
            pl.BlockSpec((1, gdim), lambda t: (0, 0)),
            pl.BlockSpec((SGU_HEADS, CHUNK, CHUNK), lambda t: (0, 0, 0)),
            pl.BlockSpec((CHUNK, SGU_HEADS), lambda t: (0, 0)),
        ],
        out_specs=[pl.BlockSpec((tm, gdim), lambda t: (gated(t), 0)), pl.BlockSpec((tm, d), lambda t: (proj(t), 0))],
        out_shape=[jax.ShapeDtypeStruct((m, gdim), BF16), jax.ShapeDtypeStruct((m, d), BF16)],
        scratch_shapes=[pltpu.VMEM((d, 2 * gdim), BF16), pltpu.VMEM((tm, 2 * gdim), F32),
                        pltpu.VMEM((tm, 2 * gdim), F32)],
        compiler_params=_params("arbitrary"),
        name="sgu_branch",
    )(x, g.reshape(1, d), w_in, ln_g.reshape(1, gdim), ln_b.reshape(1, gdim), w_s, b_s.T)


def _conv_step(h_ref, wa_ref, wg_ref, dwb_ref, b_ref, lg_ref, lb_ref, o_ref, stage_w, stage_r, *, seq):
    cw = o_ref.shape[-1]
    gd = CONV_GROUP_DIM
    w = jnp.concatenate([wa_ref[...].astype(BF16), wg_ref[...].astype(BF16)], axis=1)

    def glu(c):
        z = jnp.dot(h_ref[c * ROW_CHUNK:(c + 1) * ROW_CHUNK, :], w, preferred_element_type=F32)
        start = CONV_HALO + c * ROW_CHUNK
        u = z[:, :cw] * jax.nn.sigmoid(z[:, cw:])
        for grp in range(cw // gd):
            stage_w[grp, start:start + ROW_CHUNK, :] = u[:, grp * gd:(grp + 1) * gd]

    def conv(c):
        for s in range(ROW_CHUNK // CONV_ROWS):
            r0 = c * ROW_CHUNK + s * CONV_ROWS
            base = r0 + CONV_HALO - CONV_PAD
            for grp in range(cw // gd):
                cols = slice(grp * gd, (grp + 1) * gd)
                acc = jnp.zeros((CONV_ROWS // SUBLANES, SUBLANES, gd), F32)
                for k in range(CONV_WIDTH):
                    tap = stage_r[grp, base + k:base + k + CONV_ROWS, :].reshape(CONV_ROWS // SUBLANES, SUBLANES, gd)
                    acc = acc + tap * dwb_ref[k * SUBLANES:(k + 1) * SUBLANES, cols]
                y = _layer_norm(acc.reshape(CONV_ROWS, gd) + b_ref[:, cols], lg_ref[:, cols], lb_ref[:, cols])
                o_ref[r0:r0 + CONV_ROWS, cols] = _silu(y).astype(o_ref.dtype)

    for c in range(seq // ROW_CHUNK):
        glu(c)
        conv(c)


def _conv_branch_kernel(h_ref, wa_ref, wg_ref, dw_ref, b_ref, lg_ref, lb_ref, o_ref, stage_a, stage_b, dwb_ref, *, seq):
    t = pl.program_id(0)
    cw = o_ref.shape[-1]

    @pl.when(t == 0)
    def _():
        stage_a[...] = jnp.zeros_like(stage_a)
        stage_b[...] = jnp.zeros_like(stage_b)

    for k in range(CONV_WIDTH):
        dwb_ref[k * SUBLANES:(k + 1) * SUBLANES, :] = jnp.broadcast_to(dw_ref[k:k + 1, :], (SUBLANES, cw))

    step = functools.partial(_conv_step, h_ref, wa_ref, wg_ref, dwb_ref, b_ref, lg_ref, lb_ref, o_ref, seq=seq)

    @pl.when(t % 2 == 0)
    def _():
        step(stage_a, stage_b)

    @pl.when(t % 2 == 1)
    def _():
        step(stage_b, stage_a)


def _conv_branch(h, w_in, idx, dw_w, dw_b, ln_g, ln_b):
    bsz, seq, d = h.shape
    cdim = dw_w.shape[1]
    cw = CONV_BLOCK
    nblk = cdim // cw
    items = bsz * nblk

    def proj(t):
        return jnp.minimum(t, items - 1)

    def conv(t):
        return jnp.maximum(t - 1, 0)

    vec = pl.BlockSpec((1, cw), lambda t: (0, conv(t) % nblk))
    stage = (cw // CONV_GROUP_DIM, seq + 2 * CONV_HALO, CONV_GROUP_DIM)
    return pl.pallas_call(
        functools.partial(_conv_branch_kernel, seq=seq),
        grid=(items + 1,),
        in_specs=[
            pl.BlockSpec((None, seq, d), lambda t: (proj(t) // nblk, 0, 0)),
            pl.BlockSpec((None, d, cw), lambda t: (idx, 0, proj(t) % nblk)),
            pl.BlockSpec((None, d, cw), lambda t: (idx, 0, nblk + proj(t) % nblk)),
            pl.BlockSpec((CONV_WIDTH, cw), lambda t: (0, conv(t) % nblk)),
            vec, vec, vec,
        ],
        out_specs=pl.BlockSpec((None, seq, cw), lambda t: (conv(t) // nblk, 0, conv(t) % nblk)),
        out_shape=jax.ShapeDtypeStruct((bsz, seq, cdim), BF16),
        scratch_shapes=[pltpu.VMEM(stage, F32), pltpu.VMEM(stage, F32), pltpu.VMEM((CONV_WIDTH * SUBLANES, cw), F32)],
        compiler_params=_params("arbitrary"),
        name="conv_branch",
    )(h, w_in, w_in, dw_w, dw_b.reshape(1, cdim), ln_g.reshape(1, cdim), ln_b.reshape(1, cdim))


def _proj_residual_kernel(*refs, n_lhs, has_bias):
    lhs_refs = refs[:n_lhs]
    w_ref, x_ref = refs[n_lhs], refs[n_lhs + 1]
    o_ref, wb_ref = refs[-2], refs[-1]

    @pl.when(pl.program_id(0) == 0)
    def _():
        _cast_weight(w_ref, wb_ref)

    acc = x_ref[...]
    if has_bias:
        acc = acc + refs[n_lhs + 2][...]
    k0 = 0
    for lhs_ref in lhs_refs:
        kw = lhs_ref.shape[-1]
        acc = acc + jnp.dot(lhs_ref[...], wb_ref[k0:k0 + kw, :], preferred_element_type=F32)
        k0 += kw
    o_ref[...] = acc


def _proj_residual(lhs_list, w, idx, x, bias=None, *, tm):
    m, n = x.shape
    k = w.shape[1]
    in_specs = [pl.BlockSpec((tm, lhs.shape[1]), lambda i: (i, 0)) for lhs in lhs_list]
    in_specs += [pl.BlockSpec((None, k, n), lambda i: (idx, 0, 0), pipeline_mode=pl.Buffered(1)),
                 pl.BlockSpec((tm, n), lambda i: (i, 0))]
    args = list(lhs_list) + [w, x]
    if bias is not None:
        in_specs.append(pl.BlockSpec((1, n), lambda i: (0, 0)))
        args.append(bias.reshape(1, n))
    return pl.pallas_call(
        functools.partial(_proj_residual_kernel, n_lhs=len(lhs_list), has_bias=bias is not None),
        grid=(m // tm,),
        in_specs=in_specs,
        out_specs=pl.BlockSpec((tm, n), lambda i: (i, 0)),
        out_shape=jax.ShapeDtypeStruct((m, n), F32),
        scratch_shapes=[pltpu.VMEM((k, n), BF16)],
        compiler_params=_params("arbitrary"),
        name="proj_residual",
    )(*args)


def _ffn_kernel(x_hbm, g_ref, wg_ref, wu_ref, wd_ref, fg_ref, o_ref, *rest, final_norm):
    if final_norm:
        h_ref, x_buf, x_sem = rest
    else:
        rstd_ref, h_ref, x_buf, x_sem = rest
    i = pl.program_id(0)
    f = pl.program_id(1)
    tm = x_buf.shape[0]

    def x_copy(tile):
        return pltpu.make_async_copy(x_hbm.at[pl.ds(pl.multiple_of(tile * tm, tm), tm), :], x_buf, x_sem)

    @pl.when((i == 0) & (f == 0))
    def _():
        x_copy(0).start()

    @pl.when(f == 0)
    def _():
        x_copy(i).wait()
        x = x_buf[...]
        h_ref[...] = _rms(x, g_ref[...]).astype(BF16)
        o_ref[...] = x

    @pl.when((f == 1) & (i + 1 < pl.num_programs(0)))
    def _():
        x_copy(i + 1).start()

    tf = wg_ref.shape[-1]
    w_gu = jnp.concatenate([wg_ref[...].astype(BF16), wu_ref[...].astype(BF16)], axis=1)
    gu = jnp.dot(h_ref[...], w_gu, preferred_element_type=F32)
    act = (_silu(gu[:, :tf]) * gu[:, tf:]).astype(BF16)
    o_ref[...] += jnp.dot(act, wd_ref[...].astype(BF16), preferred_element_type=F32)

    @pl.when(f == pl.num_programs(1) - 1)
    def _():
        y = o_ref[...]
        if final_norm:
            o_ref[...] = _rms(y, fg_ref[...])
        else:
            rstd = lax.rsqrt(jnp.mean(y * y, axis=-1, keepdims=True) + RMS_EPS)
            rstd_ref[...] = jnp.broadcast_to(rstd, rstd_ref.shape)


def _ffn(x, g, w_gate, w_up, w_down, idx, final_g, *, tm, tf, final_norm):
    m, d = x.shape
    dff = w_gate.shape[2]
    out_specs = [pl.BlockSpec((tm, d), lambda i, f: (i, 0))]
    out_shape = [jax.ShapeDtypeStruct((m, d), F32)]
    if not final_norm:
        out_specs.append(pl.BlockSpec((tm, LANES), lambda i, f: (i, 0)))
        out_shape.append(jax.ShapeDtypeStruct((m, LANES), F32))
    return pl.pallas_call(
        functools.partial(_ffn_kernel, final_norm=final_norm),
        grid=(m // tm, dff // tf),
        in_specs=[
            pl.BlockSpec(memory_space=pl.ANY),
            pl.BlockSpec((1, d), lambda i, f: (0, 0)),
            pl.BlockSpec((None, d, tf), lambda i, f: (idx, 0, f)),
            pl.BlockSpec((None, d, tf), lambda i, f: (idx, 0, f)),
            pl.BlockSpec((None, tf, d), lambda i, f: (idx, f, 0)),
            pl.BlockSpec((1, d), lambda i, f: (0, 0)),
        ],
        out_specs=out_specs,
        out_shape=out_shape,
        scratch_shapes=[pltpu.VMEM((tm, d), BF16), pltpu.VMEM((tm, d), F32), pltpu.SemaphoreType.DMA(())],
        compiler_params=_params("arbitrary", "arbitrary"),
        name="ffn",
    )(x, g.reshape(1, d), w_gate, w_up, w_down, final_g.reshape(1, d))


def _fnet_mix_kernel(t_ref, cs_ref, x_ref, r_ref, g_ref, o_ref, p_ref, uv_ref, y_ref):
    q = t_ref.shape[1]
    tn = o_ref.shape[-1]
    gd = cs_ref.shape[0]

    def fold(rows):
        hq = []
        for m in range(FFT_RADIX):
            rs = pl.ds(m * q + rows.start, rows.size)
            rstd = jnp.concatenate([r_ref[rs, :]] * (tn // LANES), axis=1)
            hq.append(x_ref[rs, :] * rstd * g_ref[...])
        e, o, d, qq = hq[0] + hq[2], hq[1] + hq[3], hq[0] - hq[2], hq[1] - hq[3]
        for idx, val in enumerate((e + o, e - o, d, qq)):
            p_ref[idx, rows, :] = val.astype(BF16)

    _for_row_chunks(q, fold)

    for grp in range(tn // gd):
        cols = slice(grp * gd, (grp + 1) * gd)
        res = [jnp.dot(p_ref[idx, :, cols], cs_ref[...], preferred_element_type=F32) for idx in range(FFT_RADIX)]
        c = [r[:, :gd] for r in res]
        s = [r[:, gd:] for r in res]
        u = (c[0], c[2] - s[3], c[1], c[2] + s[3])
        v = (s[0], s[2] + c[3], s[1], s[2] - c[3])
        for r in range(FFT_RADIX):
            uv_ref[r, 0:q, cols] = u[r].astype(BF16)
            uv_ref[r, q:2 * q, cols] = v[r].astype(BF16)

    for r in range(FFT_RADIX):
        y_ref[r] = jnp.dot(t_ref[r], uv_ref[r], preferred_element_type=F32).astype(BF16)
    rows = q // FFT_RADIX
    for kt in range(FFT_RADIX):
        blk = y_ref[:, kt * rows:(kt + 1) * rows, :]
        o_ref[kt * q:(kt + 1) * q, :] = jnp.swapaxes(blk, 0, 1).reshape(q, blk.shape[-1])


def _fnet_mix(tables, cs, x, rstd, g, *, tn):
    bsz, seq, d = x.shape
    q = seq // FFT_RADIX
    gd = cs.shape[0]
    return pl.pallas_call(
        _fnet_mix_kernel,
        grid=(bsz, d // tn),
        in_specs=[
            pl.BlockSpec((FFT_RADIX, q, 2 * q), lambda bb, j: (0, 0, 0), pipeline_mode=pl.Buffered(1)),
            pl.BlockSpec((gd, 2 * gd), lambda bb, j: (0, 0)),
            pl.BlockSpec((None, seq, tn), lambda bb, j: (bb, 0, j)),
            pl.BlockSpec((None, seq, LANES), lambda bb, j: (bb, 0, 0)),
            pl.BlockSpec((1, tn), lambda bb, j: (0, j)),
        ],
        out_specs=pl.BlockSpec((None, seq, tn), lambda bb, j: (bb, 0, j)),
        out_shape=jax.ShapeDtypeStruct((bsz, seq, d), BF16),
        scratch_shapes=[pltpu.VMEM((FFT_RADIX, q, tn), BF16), pltpu.VMEM((FFT_RADIX, 2 * q, tn), BF16),
                        pltpu.VMEM((FFT_RADIX, q, tn), BF16)],
        compiler_params=_params("parallel", "parallel"),
        name="fnet_mix",
    )(tables, cs, x, rstd, g.reshape(1, d))


def _cos_sin(rows, cols, n):
    ang = ((rows[:, None] * cols[None, :]) % n).astype(F32) * (2.0 * math.pi / n)
    return jnp.cos(ang), jnp.sin(ang)


def _dft_tables(n, scale):
    q = 1 << (int(math.log2(n)) // 2)
    cols = jnp.arange(n, dtype=jnp.int32)
    ca, sa = _cos_sin(jnp.arange(n // q, dtype=jnp.int32) * q, cols, n)
    cb, sb = _cos_sin(jnp.arange(q, dtype=jnp.int32), cols, n)
    ca, sa = (ca * scale)[:, None, :], (sa * scale)[:, None, :]
    cb, sb = cb[None, :, :], sb[None, :, :]
    return (ca * cb - sa * sb).reshape(n, n), (sa * cb + ca * sb).reshape(n, n)


def _radix_tables(n):
    q = n // FFT_RADIX
    c4, s4 = _dft_tables(q, 1.0 / math.sqrt(n))
    cr, sr = _cos_sin(jnp.arange(FFT_RADIX, dtype=jnp.int32), jnp.arange(q, dtype=jnp.int32), n)
    cr, sr = cr[:, None, :], sr[:, None, :]
    c = c4[None] * cr - s4[None] * sr
    sn = s4[None] * cr + c4[None] * sr
    return jnp.concatenate([c, -sn], axis=2)


def kernel(x, mix_norm_g, ffn_norm_g, final_norm_g, ab_w_in, conv_dw_w, conv_dw_b, conv_ln_g, conv_ln_b,
           sgu_ln_g, sgu_ln_b, sgu_w, sgu_b, ab_w_out, fnet_w_out, fnet_b_out, ffn_w_gate, ffn_w_up, ffn_w_down):
    bsz, seq, d = x.shape
    m = bsz * seq
    depth = mix_norm_g.shape[0]
    xf = x.reshape(m, d)
    for layer in range(depth):
        if layer % 2 == 0:
            i = layer // 2
            y_sgu, h = _sgu_branch(xf, mix_norm_g[layer], ab_w_in, i, sgu_ln_g[i], sgu_ln_b[i], sgu_w[i], sgu_b[i],
                                   tm=SGU_ROWS)
            y_conv = _conv_branch(h.reshape(bsz, seq, d), ab_w_in, i, conv_dw_w[i], conv_dw_b[i], conv_ln_g[i],
                                  conv_ln_b[i])
            xf = _proj_residual([y_conv.reshape(m, -1), y_sgu], ab_w_out, i, xf, tm=PROJ_ROWS)
        else:
            j = layer // 2
            gd = d // FNET_GROUPS
            cc, sc = _dft_tables(gd, 1.0 / math.sqrt(gd))
            y = _fnet_mix(_radix_tables(seq).astype(BF16), jnp.concatenate([cc, sc], axis=1).astype(BF16),
                          xf.reshape(bsz, seq, d), rstd.reshape(bsz, seq, LANES), mix_norm_g[layer], tn=FNET_COLS)
            xf = _proj_residual([y.reshape(m, d)], fnet_w_out, j, xf, fnet_b_out[j], tm=PROJ_ROWS)
        outs = _ffn(xf, ffn_norm_g[layer], ffn_w_gate, ffn_w_up, ffn_w_down, layer, final_norm_g,
                    tm=FFN_ROWS, tf=FFN_HIDDEN, final_norm=layer == depth - 1)
        xf = outs[0]
        rstd = outs[1] if len(outs) > 1 else None
    return xf.reshape(bsz, seq, d)
```

```python
import functools
import math

import jax
import jax.numpy as jnp
from jax import lax
from jax.experimental import pallas as pl
from jax.experimental.pallas import tpu as pltpu

F32 = jnp.float32
BF16 = jnp.bfloat16

RMS_EPS = 1e-6
LN_EPS = 1e-5
CONV_GROUP_DIM = 128
CONV_WIDTH = 31
CONV_PAD = (CONV_WIDTH - 1) // 2
CONV_HALO = 16
CONV_ROWS = 64
CONV_BLOCK = 256
SUBLANES = 8
LANES = 128
SGU_HEADS = 8
CHUNK = 128
FNET_GROUPS = 8
FFT_RADIX = 4

V7X_VMEM_BYTES = 64 * 1024 * 1024
VMEM_LIMIT_BYTES = V7X_VMEM_BYTES - 3 * 1024 * 1024
ROW_CHUNK = 256

SGU_ROWS = 512
PROJ_ROWS = 512
FFN_ROWS = 1024
FFN_HIDDEN = 512
FNET_COLS = 1024


def _params(*semantics):
    return pltpu.CompilerParams(dimension_semantics=semantics, vmem_limit_bytes=VMEM_LIMIT_BYTES)


def _rms(x, g):
    return x * lax.rsqrt(jnp.mean(x * x, axis=-1, keepdims=True) + RMS_EPS) * g


def _layer_norm(x, g, b):
    mu = jnp.mean(x, axis=-1, keepdims=True)
    xc = x - mu
    return xc * lax.rsqrt(jnp.mean(xc * xc, axis=-1, keepdims=True) + LN_EPS) * g + b


def _gelu(x):
    return 0.5 * x * (1.0 + lax.erf(x * (1.0 / math.sqrt(2.0))))


def _silu(x):
    return x * jax.nn.sigmoid(x)


def _for_row_chunks(nrows, fn):
    def body(r, carry):
        fn(pl.ds(pl.multiple_of(r * ROW_CHUNK, ROW_CHUNK), ROW_CHUNK))
        return carry
    lax.fori_loop(0, nrows // ROW_CHUNK, body, 0)


def _cast_weight(w_ref, wb_ref):
    def cast(rows):
        wb_ref[rows, :] = w_ref[rows, :].astype(BF16)
    _for_row_chunks(w_ref.shape[0], cast)


def _sgu_branch_kernel(x_ref, g_ref, w_ref, lg_ref, lb_ref, ws_ref, bs_ref, o_ref, h_ref, wb_ref, z_a, z_b, *, tm, nt):
    t = pl.program_id(0)
    gdim = o_ref.shape[-1]
    hd = gdim // SGU_HEADS

    def project(z_w):
        for r in range(tm // ROW_CHUNK):
            rows = slice(r * ROW_CHUNK, (r + 1) * ROW_CHUNK)
            h = _rms(x_ref[rows, :], g_ref[...]).astype(BF16)
            h_ref[rows, :] = h
            z_w[rows, :] = jnp.dot(h, wb_ref[...], preferred_element_type=F32)

    def gate(z_r):
        for c in range(tm // CHUNK):
            rows = slice(c * CHUNK, (c + 1) * CHUNK)
            v = _layer_norm(_gelu(z_r[rows, gdim:]), lg_ref[...], lb_ref[...]).astype(BF16)
            for hh in range(SGU_HEADS):
                cols = slice(hh * hd, (hh + 1) * hd)
                mixed = jnp.dot(ws_ref[hh].astype(BF16), v[:, cols], preferred_element_type=F32)
                mixed = mixed + bs_ref[:, hh:hh + 1]
                o_ref[rows, cols] = (_gelu(z_r[rows, cols]) * mixed).astype(o_ref.dtype)

    @pl.when(t == 0)
    def _():
        _cast_weight(w_ref, wb_ref)
        project(z_a)

    for parity, (z_w, z_r) in enumerate(((z_a, z_b), (z_b, z_a))):
        @pl.when((t > 0) & (t < nt) & (t % 2 == parity))
        def _():
            project(z_w)
            gate(z_r)

    @pl.when(t == nt)
    def _():
        gate(z_a if nt % 2 == 1 else z_b)


def _sgu_branch(x, g, w_in, idx, ln_g, ln_b, w_s, b_s, *, tm):
    m, d = x.shape
    gdim = ln_g.shape[0]
    wblk = w_in.shape[2] // (2 * gdim) - 1
    nt = m // tm

    def proj(t):
        return jnp.minimum(t, nt - 1)

    def gated(t):
        return jnp.maximum(t - 1, 0)

    return pl.pallas_call(
        functools.partial(_sgu_branch_kernel, tm=tm, nt=nt),
        grid=(nt + 1,),
        in_specs=[
            pl.BlockSpec((tm, d), lambda t: (proj(t), 0)),
            pl.BlockSpec((1, d), lambda t: (0, 0)),
            pl.BlockSpec((None, d, 2 * gdim), lambda t: (idx, 0, wblk), pipeline_mode=pl.Buffered(1)),
            pl.BlockSpec((1, gdim), lambda t: (0, 0)),
            pl.BlockSpec((1, gdim), lambda t: (0, 0)),
            pl.BlockSpec((SGU_HEADS, CHUNK, CHUNK), lambda t: (0, 0, 0)),
            pl.BlockSpec((CHUNK, SGU_HEADS), lambda t: (0, 0)),
        ],
        out_specs=[pl.BlockSpec((tm, gdim), lambda t: (gated(t), 0)), pl.BlockSpec((tm, d), lambda t: (proj(t), 0))],
        out_shape=[jax.ShapeDtypeStruct((m, gdim), BF16), jax.ShapeDtypeStruct((m, d), BF16)],
        scratch_shapes=[pltpu.VMEM((d, 2 * gdim), BF16), pltpu.VMEM((tm, 2 * gdim), F32),
                        pltpu.VMEM((tm, 2 * gdim), F32)],
        compiler_params=_params("arbitrary"),
        name="sgu_branch",
    )(x, g.reshape(1, d), w_in, ln_g.reshape(1, gdim), ln_b.reshape(1, gdim), w_s, b_s.T)


def _conv_step(h_ref, wa_ref, wg_ref, dwb_ref, b_ref, lg_ref, lb_ref, o_ref, stage_w, stage_r, *, seq):
    cw = o_ref.shape[-1]
    gd = CONV_GROUP_DIM
    w = jnp.concatenate([wa_ref[...].astype(BF16), wg_ref[...].astype(BF16)], axis=1)

    def glu(c):
        z = jnp.dot(h_ref[c * ROW_CHUNK:(c + 1) * ROW_CHUNK, :], w, preferred_element_type=F32)
        start = CONV_HALO + c * ROW_CHUNK
        u = z[:, :cw] * jax.nn.sigmoid(z[:, cw:])
        for grp in range(cw // gd):
            stage_w[grp, start:start + ROW_CHUNK, :] = u[:, grp * gd:(grp + 1) * gd]

    def conv(c):
        for s in range(ROW_CHUNK // CONV_ROWS):
            r0 = c * ROW_CHUNK + s * CONV_ROWS
            base = r0 + CONV_HALO - CONV_PAD
            for grp in range(cw // gd):
                cols = slice(grp * gd, (grp + 1) * gd)
                acc = jnp.zeros((CONV_ROWS // SUBLANES, SUBLANES, gd), F32)
                for k in range(CONV_WIDTH):
                    tap = stage_r[grp, base + k:base + k + CONV_ROWS, :].reshape(CONV_ROWS // SUBLANES, SUBLANES, gd)
                    acc = acc + tap * dwb_ref[k * SUBLANES:(k + 1) * SUBLANES, cols]
                y = _layer_norm(acc.reshape(CONV_ROWS, gd) + b_ref[:, cols], lg_ref[:, cols], lb_ref[:, cols])
                o_ref[r0:r0 + CONV_ROWS, cols] = _silu(y).astype(o_ref.dtype)

    for c in range(seq // ROW_CHUNK):
        glu(c)
        conv(c)


def _conv_branch_kernel(h_ref, wa_ref, wg_ref, dw_ref, b_ref, lg_ref, lb_ref, o_ref, stage_a, stage_b, dwb_ref, *, seq):
    t = pl.program_id(0)
    cw = o_ref.shape[-1]

    @pl.when(t == 0)
    def _():
        stage_a[...] = jnp.zeros_like(stage_a)
        stage_b[...] = jnp.zeros_like(stage_b)

    for k in range(CONV_WIDTH):
        dwb_ref[k * SUBLANES:(k + 1) * SUBLANES, :] = jnp.broadcast_to(dw_ref[k:k + 1, :], (SUBLANES, cw))

    step = functools.partial(_conv_step, h_ref, wa_ref, wg_ref, dwb_ref, b_ref, lg_ref, lb_ref, o_ref, seq=seq)

    @pl.when(t % 2 == 0)
    def _():
        step(stage_a, stage_b)

    @pl.when(t % 2 == 1)
    def _():
        step(stage_b, stage_a)


def _conv_branch(h, w_in, idx, dw_w, dw_b, ln_g, ln_b):
    bsz, seq, d = h.shape
    cdim = dw_w.shape[1]
    cw = CONV_BLOCK
    nblk = cdim // cw
    items = bsz * nblk

    def proj(t):
        return jnp.minimum(t, items - 1)

    def conv(t):
        return jnp.maximum(t - 1, 0)

    vec = pl.BlockSpec((1, cw), lambda t: (0, conv(t) % nblk))
    stage = (cw // CONV_GROUP_DIM, seq + 2 * CONV_HALO, CONV_GROUP_DIM)
    return pl.pallas_call(
        functools.partial(_conv_branch_kernel, seq=seq),
        grid=(items + 1,),
        in_specs=[
            pl.BlockSpec((None, seq, d), lambda t: (proj(t) // nblk, 0, 0)),
            pl.BlockSpec((None, d, cw), lambda t: (idx, 0, proj(t) % nblk)),
            pl.BlockSpec((None, d, cw), lambda t: (idx, 0, nblk + proj(t) % nblk)),
            pl.BlockSpec((CONV_WIDTH, cw), lambda t: (0, conv(t) % nblk)),
            vec, vec, vec,
        ],
        out_specs=pl.BlockSpec((None, seq, cw), lambda t: (conv(t) // nblk, 0, conv(t) % nblk)),
        out_shape=jax.ShapeDtypeStruct((bsz, seq, cdim), BF16),
        scratch_shapes=[pltpu.VMEM(stage, F32), pltpu.VMEM(stage, F32), pltpu.VMEM((CONV_WIDTH * SUBLANES, cw), F32)],
        compiler_params=_params("arbitrary"),
        name="conv_branch",
    )(h, w_in, w_in, dw_w, dw_b.reshape(1, cdim), ln_g.reshape(1, cdim), ln_b.reshape(1, cdim))


def _proj_residual_kernel(*refs, n_lhs, has_bias):
    lhs_refs = refs[:n_lhs]
    w_ref, x_ref = refs[n_lhs], refs[n_lhs + 1]
    o_ref, wb_ref = refs[-2], refs[-1]

    @pl.when(pl.program_id(0) == 0)
    def _():
        _cast_weight(w_ref, wb_ref)

    acc = x_ref[...]
    if has_bias:
        acc = acc + refs[n_lhs + 2][...]
    k0 = 0
    for lhs_ref in lhs_refs:
        kw = lhs_ref.shape[-1]
        acc = acc + jnp.dot(lhs_ref[...], wb_ref[k0:k0 + kw, :], preferred_element_type=F32)
        k0 += kw
    o_ref[...] = acc


def _proj_residual(lhs_list, w, idx, x, bias=None, *, tm):
    m, n = x.shape
    k = w.shape[1]
    in_specs = [pl.BlockSpec((tm, lhs.shape[1]), lambda i: (i, 0)) for lhs in lhs_list]
    in_specs += [pl.BlockSpec((None, k, n), lambda i: (idx, 0, 0), pipeline_mode=pl.Buffered(1)),
                 pl.BlockSpec((tm, n), lambda i: (i, 0))]
    args = list(lhs_list) + [w, x]
    if bias is not None:
        in_specs.append(pl.BlockSpec((1, n), lambda i: (0, 0)))
        args.append(bias.reshape(1, n))
    return pl.pallas_call(
        functools.partial(_proj_residual_kernel, n_lhs=len(lhs_list), has_bias=bias is not None),
        grid=(m // tm,),
        in_specs=in_specs,
        out_specs=pl.BlockSpec((tm, n), lambda i: (i, 0)),
        out_shape=jax.ShapeDtypeStruct((m, n), F32),
        scratch_shapes=[pltpu.VMEM((k, n), BF16)],
        compiler_params=_params("arbitrary"),
        name="proj_residual",
    )(*args)


def _ffn_kernel(x_hbm, g_ref, wg_ref, wu_ref, wd_ref, fg_ref, o_ref, *rest, final_norm):
    if final_norm:
        h_ref, x_buf, x_sem = rest
    else:
        rstd_ref, h_ref, x_buf, x_sem = rest
    i = pl.program_id(0)
    f = pl.program_id(1)
    tm = x_buf.shape[0]

    def x_copy(tile):
        return pltpu.make_async_copy(x_hbm.at[pl.ds(pl.multiple_of(tile * tm, tm), tm), :], x_buf, x_sem)

    @pl.when((i == 0) & (f == 0))
    def _():
        x_copy(0).start()

    last = pl.num_programs(1) - 1

    def delta():
        tf = wg_ref.shape[-1]
        w_gu = jnp.concatenate([wg_ref[...].astype(BF16), wu_ref[...].astype(BF16)], axis=1)
        gu = jnp.dot(h_ref[...], w_gu, preferred_element_type=F32)
        act = (_silu(gu[:, :tf]) * gu[:, tf:]).astype(BF16)
        return jnp.dot(act, wd_ref[...].astype(BF16), preferred_element_type=F32)

    @pl.when(f == 0)
    def _():
        x_copy(i).wait()
        h_ref[...] = _rms(x_buf[...], g_ref[...]).astype(BF16)
        o_ref[...] = x_buf[...] + delta()

    @pl.when((f == 1) & (i + 1 < pl.num_programs(0)))
    def _():
        x_copy(i + 1).start()

    if final_norm:
        @pl.when(f > 0)
        def _():
            o_ref[...] += delta()

        @pl.when(f == last)
        def _():
            o_ref[...] = _rms(o_ref[...], fg_ref[...])
    else:
        @pl.when((f > 0) & (f < last))
        def _():
            o_ref[...] += delta()

        @pl.when(f == last)
        def _():
            y = o_ref[...] + delta()
            o_ref[...] = y
            rstd = lax.rsqrt(jnp.mean(y * y, axis=-1, keepdims=True) + RMS_EPS)
            rstd_ref[...] = jnp.broadcast_to(rstd, rstd_ref.shape)


def _ffn(x, g, w_gate, w_up, w_down, idx, final_g, *, tm, tf, final_norm):
    m, d = x.shape
    dff = w_gate.shape[2]
    assert dff // tf >= 2, "the first and the last hidden step must be different steps"
    out_specs = [pl.BlockSpec((tm, d), lambda i, f: (i, 0))]
    out_shape = [jax.ShapeDtypeStruct((m, d), F32)]
    if not final_norm:
        out_specs.append(pl.BlockSpec((tm, LANES), lambda i, f: (i, 0)))
        out_shape.append(jax.ShapeDtypeStruct((m, LANES), F32))
    return pl.pallas_call(
        functools.partial(_ffn_kernel, final_norm=final_norm),
        grid=(m // tm, dff // tf),
        in_specs=[
            pl.BlockSpec(memory_space=pl.ANY),
            pl.BlockSpec((1, d), lambda i, f: (0, 0)),
            pl.BlockSpec((None, d, tf), lambda i, f: (idx, 0, f)),
            pl.BlockSpec((None, d, tf), lambda i, f: (idx, 0, f)),
            pl.BlockSpec((None, tf, d), lambda i, f: (idx, f, 0)),
            pl.BlockSpec((1, d), lambda i, f: (0, 0)),
        ],
        out_specs=out_specs,
        out_shape=out_shape,
        scratch_shapes=[pltpu.VMEM((tm, d), BF16), pltpu.VMEM((tm, d), F32), pltpu.SemaphoreType.DMA(())],
        compiler_params=_params("arbitrary", "arbitrary"),
        name="ffn",
    )(x, g.reshape(1, d), w_gate, w_up, w_down, final_g.reshape(1, d))


def _fnet_mix_kernel(t_ref, cs_ref, x_ref, r_ref, g_ref, o_ref, p_ref, uv_ref, y_ref):
    q = t_ref.shape[1]
    tn = o_ref.shape[-1]
    gd = cs_ref.shape[0]

    def fold(rows):
        hq = []
        for m in range(FFT_RADIX):
            rs = pl.ds(m * q + rows.start, rows.size)
            rstd = jnp.concatenate([r_ref[rs, :]] * (tn // LANES), axis=1)
            hq.append(x_ref[rs, :] * rstd * g_ref[...])
        e, o, d, qq = hq[0] + hq[2], hq[1] + hq[3], hq[0] - hq[2], hq[1] - hq[3]
        for idx, val in enumerate((e + o, e - o, d, qq)):
            p_ref[idx, rows, :] = val.astype(BF16)

    _for_row_chunks(q, fold)

    for grp in range(tn // gd):
        cols = slice(grp * gd, (grp + 1) * gd)
        res = [jnp.dot(p_ref[idx, :, cols], cs_ref[...], preferred_element_type=F32) for idx in range(FFT_RADIX)]
        c = [r[:, :gd] for r in res]
        s = [r[:, gd:] for r in res]
        u = (c[0], c[2] - s[3], c[1], c[2] + s[3])
        v = (s[0], s[2] + c[3], s[1], s[2] - c[3])
        for r in range(FFT_RADIX):
            uv_ref[r, 0:q, cols] = u[r].astype(BF16)
            uv_ref[r, q:2 * q, cols] = v[r].astype(BF16)

    for r in range(FFT_RADIX):
        y_ref[r] = jnp.dot(t_ref[r], uv_ref[r], preferred_element_type=F32).astype(BF16)
    rows = q // FFT_RADIX
    for kt in range(FFT_RADIX):
        blk = y_ref[:, kt * rows:(kt + 1) * rows, :]
        o_ref[kt * q:(kt + 1) * q, :] = jnp.swapaxes(blk, 0, 1).reshape(q, blk.shape[-1])


def _fnet_mix(tables, cs, x, rstd, g, *, tn):
    bsz, seq, d = x.shape
    q = seq // FFT_RADIX
    gd = cs.shape[0]
    return pl.pallas_call(
        _fnet_mix_kernel,
        grid=(bsz, d // tn),
        in_specs=[
            pl.BlockSpec((FFT_RADIX, q, 2 * q), lambda bb, j: (0, 0, 0), pipeline_mode=pl.Buffered(1)),
            pl.BlockSpec((gd, 2 * gd), lambda bb, j: (0, 0)),
            pl.BlockSpec((None, seq, tn), lambda bb, j: (bb, 0, j)),
            pl.BlockSpec((None, seq, LANES), lambda bb, j: (bb, 0, 0)),
            pl.BlockSpec((1, tn), lambda bb, j: (0, j)),
        ],
        out_specs=pl.BlockSpec((None, seq, tn), lambda bb, j: (bb, 0, j)),
        out_shape=jax.ShapeDtypeStruct((bsz, seq, d), BF16),
        scratch_shapes=[pltpu.VMEM((FFT_RADIX, q, tn), BF16), pltpu.VMEM((FFT_RADIX, 2 * q, tn), BF16),
                        pltpu.VMEM((FFT_RADIX, q, tn), BF16)],
        compiler_params=_params("parallel", "parallel"),
        name="fnet_mix",
    )(tables, cs, x, rstd, g.reshape(1, d))


def _cos_sin(rows, cols, n):
    ang = ((rows[:, None] * cols[None, :]) % n).astype(F32) * (2.0 * math.pi / n)
    return jnp.cos(ang), jnp.sin(ang)


def _dft_tables(n, scale):
    q = 1 << (int(math.log2(n)) // 2)
    cols = jnp.arange(n, dtype=jnp.int32)
    ca, sa = _cos_sin(jnp.arange(n // q, dtype=jnp.int32) * q, cols, n)
    cb, sb = _cos_sin(jnp.arange(q, dtype=jnp.int32), cols, n)
    ca, sa = (ca * scale)[:, None, :], (sa * scale)[:, None, :]
    cb, sb = cb[None, :, :], sb[None, :, :]
    return (ca * cb - sa * sb).reshape(n, n), (sa * cb + ca * sb).reshape(n, n)


def _radix_tables(n):
    q = n // FFT_RADIX
    c4, s4 = _dft_tables(q, 1.0 / math.sqrt(n))
    cr, sr = _cos_sin(jnp.arange(FFT_RADIX, dtype=jnp.int32), jnp.arange(q, dtype=jnp.int32), n)
    cr, sr = cr[:, None, :], sr[:, None, :]
    c = c4[None] * cr - s4[None] * sr
    sn = s4[None] * cr + c4[None] * sr
    return jnp.concatenate([c, -sn], axis=2)


def kernel(x, mix_norm_g, ffn_norm_g, final_norm_g, ab_w_in, conv_dw_w, conv_dw_b, conv_ln_g, conv_ln_b,
           sgu_ln_g, sgu_ln_b, sgu_w, sgu_b, ab_w_out, fnet_w_out, fnet_b_out, ffn_w_gate, ffn_w_up, ffn_w_down):
    bsz, seq, d = x.shape
    m = bsz * seq
    depth = mix_norm_g.shape[0]
    xf = x.reshape(m, d)
    for layer in range(depth):
        if layer % 2 == 0:
            i = layer // 2
            y_sgu, h = _sgu_branch(xf, mix_norm_g[layer], ab_w_in, i, sgu_ln_g[i], sgu_ln_b[i], sgu_w[i], sgu_b[i],
                                   tm=SGU_ROWS)
            y_conv = _conv_branch(h.reshape(bsz, seq, d), ab_w_in, i, conv_dw_w[i], conv_dw_b[i], conv_ln_g[i],
                                  conv_ln_b[i])
            xf = _proj_residual([y_conv.reshape(m, -1), y_sgu], ab_w_out, i, xf, tm=PROJ_ROWS)
        else:
            j = layer // 2
            gd = d // FNET_GROUPS
            cc, sc = _dft_tables(gd, 1.0 / math.sqrt(gd))
            y = _fnet_mix(_radix_tables(seq).astype(BF16), jnp.concatenate([cc, sc], axis=1).astype(BF16),
                          xf.reshape(bsz, seq, d), rstd.reshape(bsz, seq, LANES), mix_norm_g[layer], tn=FNET_COLS)
            xf = _proj_residual([y.reshape(m, d)], fnet_w_out, j, xf, fnet_b_out[j], tm=PROJ_ROWS)
        outs = _ffn(xf, ffn_norm_g[layer], ffn_w_gate, ffn_w_up, ffn_w_down, layer, final_norm_g,
                    tm=FFN_ROWS, tf=FFN_HIDDEN, final_norm=layer == depth - 1)
        xf = outs[0]
        rstd = outs[1] if len(outs) > 1 else None
    return xf.reshape(bsz, seq, d)
```

```python
import functools
import math

import jax
import jax.numpy as jnp
from jax import lax
from jax.experimental import pallas as pl
from jax.experimental.pallas import tpu as pltpu

F32 = jnp.float32
BF16 = jnp.bfloat16

RMS_EPS = 1e-6
LN_EPS = 1e-5
CONV_GROUP_DIM = 128
CONV_WIDTH = 31
CONV_PAD = (CONV_WIDTH - 1) // 2
CONV_HALO = 16
CONV_ROWS = 64
CONV_BLOCK = 256
SUBLANES = 8
LANES = 128
SGU_HEADS = 8
CHUNK = 128
FNET_GROUPS = 8
FFT_RADIX = 4

V7X_VMEM_BYTES = 64 * 1024 * 1024
VMEM_LIMIT_BYTES = V7X_VMEM_BYTES - 3 * 1024 * 1024
ROW_CHUNK = 256

SGU_ROWS = 512
PROJ_ROWS = 512
FFN_ROWS = 1024
FFN_HIDDEN = 512
FNET_COLS = 1024


def _params(*semantics):
    return pltpu.CompilerParams(dimension_semantics=semantics, vmem_limit_bytes=VMEM_LIMIT_BYTES)


def _rms(x, g):
    return x * lax.rsqrt(jnp.mean(x * x, axis=-1, keepdims=True) + RMS_EPS) * g


def _layer_norm(x, g, b):
    mu = jnp.mean(x, axis=-1, keepdims=True)
    xc = x - mu
    return xc * lax.rsqrt(jnp.mean(xc * xc, axis=-1, keepdims=True) + LN_EPS) * g + b


def _gelu(x):
    return 0.5 * x * (1.0 + lax.erf(x * (1.0 / math.sqrt(2.0))))


def _silu(x):
    return x * jax.nn.sigmoid(x)


def _for_row_chunks(nrows, fn):
    def body(r, carry):
        fn(pl.ds(pl.multiple_of(r * ROW_CHUNK, ROW_CHUNK), ROW_CHUNK))
        return carry
    lax.fori_loop(0, nrows // ROW_CHUNK, body, 0)


def _cast_weight(w_ref, wb_ref):
    def cast(rows):
        wb_ref[rows, :] = w_ref[rows, :].astype(BF16)
    _for_row_chunks(w_ref.shape[0], cast)


def _sgu_branch_kernel(x_ref, g_ref, w_ref, lg_ref, lb_ref, ws_ref, bs_ref, o_ref, h_ref, wb_ref, z_a, z_b, *, tm, nt):
    t = pl.program_id(0)
    gdim = o_ref.shape[-1]
    hd = gdim // SGU_HEADS

    def project(z_w):
        for r in range(tm // ROW_CHUNK):
            rows = slice(r * ROW_CHUNK, (r + 1) * ROW_CHUNK)
            h = _rms(x_ref[rows, :], g_ref[...]).astype(BF16)
            h_ref[rows, :] = h
            z_w[rows, :] = jnp.dot(h, wb_ref[...], preferred_element_type=F32)

    def gate(z_r):
        for c in range(tm // CHUNK):
            rows = slice(c * CHUNK, (c + 1) * CHUNK)
            v = _layer_norm(_gelu(z_r[rows, gdim:]), lg_ref[...], lb_ref[...]).astype(BF16)
            for hh in range(SGU_HEADS):
                cols = slice(hh * hd, (hh + 1) * hd)
                mixed = jnp.dot(ws_ref[hh].astype(BF16), v[:, cols], preferred_element_type=F32)
                mixed = mixed + bs_ref[:, hh:hh + 1]
                o_ref[rows, cols] = (_gelu(z_r[rows, cols]) * mixed).astype(o_ref.dtype)

    @pl.when(t == 0)
    def _():
        _cast_weight(w_ref, wb_ref)
        project(z_a)

    for parity, (z_w, z_r) in enumerate(((z_a, z_b), (z_b, z_a))):
        @pl.when((t > 0) & (t < nt) & (t % 2 == parity))
        def _():
            project(z_w)
            gate(z_r)

    @pl.when(t == nt)
    def _():
        gate(z_a if nt % 2 == 1 else z_b)


def _sgu_branch(x, g, w_in, idx, ln_g, ln_b, w_s, b_s, *, tm):
    m, d = x.shape
    gdim = ln_g.shape[0]
    wblk = w_in.shape[2] // (2 * gdim) - 1
    nt = m // tm

    def proj(t):
        return jnp.minimum(t, nt - 1)

    def gated(t):
        return jnp.maximum(t - 1, 0)

    return pl.pallas_call(
        functools.partial(_sgu_branch_kernel, tm=tm, nt=nt),
        grid=(nt + 1,),
        in_specs=[
            pl.BlockSpec((tm, d), lambda t: (proj(t), 0)),
            pl.BlockSpec((1, d), lambda t: (0, 0)),
            pl.BlockSpec((None, d, 2 * gdim), lambda t: (idx, 0, wblk), pipeline_mode=pl.Buffered(1)),
            pl.BlockSpec((1, gdim), lambda t: (0, 0)),
            pl.BlockSpec((1, gdim), lambda t: (0, 0)),
            pl.BlockSpec((SGU_HEADS, CHUNK, CHUNK), lambda t: (0, 0, 0)),
            pl.BlockSpec((CHUNK, SGU_HEADS), lambda t: (0, 0)),
        ],
        out_specs=[pl.BlockSpec((tm, gdim), lambda t: (gated(t), 0)), pl.BlockSpec((tm, d), lambda t: (proj(t), 0))],
        out_shape=[jax.ShapeDtypeStruct((m, gdim), BF16), jax.ShapeDtypeStruct((m, d), BF16)],
        scratch_shapes=[pltpu.VMEM((d, 2 * gdim), BF16), pltpu.VMEM((tm, 2 * gdim), F32),
                        pltpu.VMEM((tm, 2 * gdim), F32)],
        compiler_params=_params("arbitrary"),
        name="sgu_branch",
    )(x, g.reshape(1, d), w_in, ln_g.reshape(1, gdim), ln_b.reshape(1, gdim), w_s, b_s.T)


def _conv_step(h_ref, wa_ref, wg_ref, dwb_ref, b_ref, lg_ref, lb_ref, o_ref, stage_w, stage_r, *, seq):
    cw = o_ref.shape[-1]
    gd = CONV_GROUP_DIM
    w = jnp.concatenate([wa_ref[...].astype(BF16), wg_ref[...].astype(BF16)], axis=1)

    def glu(c):
        z = jnp.dot(h_ref[c * ROW_CHUNK:(c + 1) * ROW_CHUNK, :], w, preferred_element_type=F32)
        start = CONV_HALO + c * ROW_CHUNK
        u = z[:, :cw] * jax.nn.sigmoid(z[:, cw:])
        for grp in range(cw // gd):
            stage_w[grp, start:start + ROW_CHUNK, :] = u[:, grp * gd:(grp + 1) * gd]

    def conv(c):
        for s in range(ROW_CHUNK // CONV_ROWS):
            r0 = c * ROW_CHUNK + s * CONV_ROWS
            base = r0 + CONV_HALO - CONV_PAD
            for grp in range(cw // gd):
                cols = slice(grp * gd, (grp + 1) * gd)
                acc = jnp.zeros((CONV_ROWS // SUBLANES, SUBLANES, gd), F32)
                for k in range(CONV_WIDTH):
                    tap = stage_r[grp, base + k:base + k + CONV_ROWS, :].reshape(CONV_ROWS // SUBLANES, SUBLANES, gd)
                    acc = acc + tap * dwb_ref[k * SUBLANES:(k + 1) * SUBLANES, cols]
                y = _layer_norm(acc.reshape(CONV_ROWS, gd) + b_ref[:, cols], lg_ref[:, cols], lb_ref[:, cols])
                o_ref[r0:r0 + CONV_ROWS, cols] = _silu(y).astype(o_ref.dtype)

    for c in range(seq // ROW_CHUNK):
        glu(c)
        conv(c)


def _conv_branch_kernel(*refs, seq, n_cast):
    h_ref, wa_ref, wg_ref, dw_ref, b_ref, lg_ref, lb_ref = refs[:7]
    cast_in = refs[7:7 + n_cast]
    o_ref = refs[7 + n_cast]
    cast_out = refs[8 + n_cast:8 + 2 * n_cast]
    stage_a, stage_b, dwb_ref = refs[8 + 2 * n_cast:]
    t = pl.program_id(0)
    cw = o_ref.shape[-1]

    @pl.when(t == 0)
    def _():
        stage_a[...] = jnp.zeros_like(stage_a)
        stage_b[...] = jnp.zeros_like(stage_b)

    for src, dst in zip(cast_in, cast_out):
        dst[...] = src[...].astype(BF16)

    for k in range(CONV_WIDTH):
        dwb_ref[k * SUBLANES:(k + 1) * SUBLANES, :] = jnp.broadcast_to(dw_ref[k:k + 1, :], (SUBLANES, cw))

    step = functools.partial(_conv_step, h_ref, wa_ref, wg_ref, dwb_ref, b_ref, lg_ref, lb_ref, o_ref, seq=seq)

    @pl.when(t % 2 == 0)
    def _():
        step(stage_a, stage_b)

    @pl.when(t % 2 == 1)
    def _():
        step(stage_b, stage_a)


def _conv_branch(h, w_in, idx, dw_w, dw_b, ln_g, ln_b, cast_weights):
    bsz, seq, d = h.shape
    cdim = dw_w.shape[1]
    cw = CONV_BLOCK
    nblk = cdim // cw
    items = bsz * nblk

    def proj(t):
        return jnp.minimum(t, items - 1)

    def conv(t):
        return jnp.maximum(t - 1, 0)

    vec = pl.BlockSpec((1, cw), lambda t: (0, conv(t) % nblk))
    stage = (cw // CONV_GROUP_DIM, seq + 2 * CONV_HALO, CONV_GROUP_DIM)
    cast_in_specs, cast_out_specs, cast_out_shapes = [], [], []
    for w, li in cast_weights:
        _, k, n = w.shape
        cast_in_specs.append(pl.BlockSpec((None, k // items, n), lambda t, li=li: (li, proj(t), 0)))
        cast_out_specs.append(pl.BlockSpec((k // items, n), lambda t: (proj(t), 0)))
        cast_out_shapes.append(jax.ShapeDtypeStruct((k, n), BF16))
    outs = pl.pallas_call(
        functools.partial(_conv_branch_kernel, seq=seq, n_cast=len(cast_weights)),
        grid=(items + 1,),
        in_specs=[
            pl.BlockSpec((None, seq, d), lambda t: (proj(t) // nblk, 0, 0)),
            pl.BlockSpec((None, d, cw), lambda t: (idx, 0, proj(t) % nblk)),
            pl.BlockSpec((None, d, cw), lambda t: (idx, 0, nblk + proj(t) % nblk)),
            pl.BlockSpec((CONV_WIDTH, cw), lambda t: (0, conv(t) % nblk)),
            vec, vec, vec,
        ] + cast_in_specs,
        out_specs=[pl.BlockSpec((None, seq, cw), lambda t: (conv(t) // nblk, 0, conv(t) % nblk))] + cast_out_specs,
        out_shape=[jax.ShapeDtypeStruct((bsz, seq, cdim), BF16)] + cast_out_shapes,
        scratch_shapes=[pltpu.VMEM(stage, F32), pltpu.VMEM(stage, F32), pltpu.VMEM((CONV_WIDTH * SUBLANES, cw), F32)],
        compiler_params=_params("arbitrary"),
        name="conv_branch",
    )(h, w_in, w_in, dw_w, dw_b.reshape(1, cdim), ln_g.reshape(1, cdim), ln_b.reshape(1, cdim),
      *[w for w, _ in cast_weights])
    return outs[0], outs[1:]


def _proj_residual_kernel(*refs, n_lhs, has_bias):
    lhs_refs = refs[:n_lhs]
    w_ref, x_ref = refs[n_lhs], refs[n_lhs + 1]
    o_ref = refs[-1]
    acc = x_ref[...]
    if has_bias:
        acc = acc + refs[n_lhs + 2][...]
    k0 = 0
    for lhs_ref in lhs_refs:
        kw = lhs_ref.shape[-1]
        acc = acc + jnp.dot(lhs_ref[...], w_ref[k0:k0 + kw, :], preferred_element_type=F32)
        k0 += kw
    o_ref[...] = acc


def _proj_residual(lhs_list, w, x, bias=None, *, tm):
    m, n = x.shape
    k = w.shape[0]
    in_specs = [pl.BlockSpec((tm, lhs.shape[1]), lambda i: (i, 0)) for lhs in lhs_list]
    in_specs += [pl.BlockSpec((k, n), lambda i: (0, 0), pipeline_mode=pl.Buffered(1)),
                 pl.BlockSpec((tm, n), lambda i: (i, 0))]
    args = list(lhs_list) + [w, x]
    if bias is not None:
        in_specs.append(pl.BlockSpec((1, n), lambda i: (0, 0)))
        args.append(bias.reshape(1, n))
    return pl.pallas_call(
        functools.partial(_proj_residual_kernel, n_lhs=len(lhs_list), has_bias=bias is not None),
        grid=(m // tm,),
        in_specs=in_specs,
        out_specs=pl.BlockSpec((tm, n), lambda i: (i, 0)),
        out_shape=jax.ShapeDtypeStruct((m, n), F32),
        compiler_params=_params("parallel"),
        name="proj_residual",
    )(*args)


def _ffn_kernel(x_hbm, g_ref, wg_ref, wu_ref, wd_ref, fg_ref, o_ref, *rest, final_norm):
    if final_norm:
        h_ref, x_buf, x_sem = rest
    else:
        rstd_ref, h_ref, x_buf, x_sem = rest
    i = pl.program_id(0)
    f = pl.program_id(1)
    tm = x_buf.shape[0]

    def x_copy(tile):
        return pltpu.make_async_copy(x_hbm.at[pl.ds(pl.multiple_of(tile * tm, tm), tm), :], x_buf, x_sem)

    @pl.when((i == 0) & (f == 0))
    def _():
        x_copy(0).start()

    last = pl.num_programs(1) - 1

    def delta():
        tf = wg_ref.shape[-1]
        w_gu = jnp.concatenate([wg_ref[...].astype(BF16), wu_ref[...].astype(BF16)], axis=1)
        gu = jnp.dot(h_ref[...], w_gu, preferred_element_type=F32)
        act = (_silu(gu[:, :tf]) * gu[:, tf:]).astype(BF16)
        return jnp.dot(act, wd_ref[...].astype(BF16), preferred_element_type=F32)

    @pl.when(f == 0)
    def _():
        x_copy(i).wait()
        h_ref[...] = _rms(x_buf[...], g_ref[...]).astype(BF16)
        o_ref[...] = x_buf[...] + delta()

    @pl.when((f == 1) & (i + 1 < pl.num_programs(0)))
    def _():
        x_copy(i + 1).start()

    if final_norm:
        @pl.when(f > 0)
        def _():
            o_ref[...] += delta()

        @pl.when(f == last)
        def _():
            o_ref[...] = _rms(o_ref[...], fg_ref[...])
    else:
        @pl.when((f > 0) & (f < last))
        def _():
            o_ref[...] += delta()

        @pl.when(f == last)
        def _():
            y = o_ref[...] + delta()
            o_ref[...] = y
            rstd = lax.rsqrt(jnp.mean(y * y, axis=-1, keepdims=True) + RMS_EPS)
            rstd_ref[...] = jnp.broadcast_to(rstd, rstd_ref.shape)


def _ffn(x, g, w_gate, w_up, w_down, idx, final_g, *, tm, tf, final_norm):
    m, d = x.shape
    dff = w_gate.shape[2]
    assert dff // tf >= 2, "the first and the last hidden step must be different steps"
    out_specs = [pl.BlockSpec((tm, d), lambda i, f: (i, 0))]
    out_shape = [jax.ShapeDtypeStruct((m, d), F32)]
    if not final_norm:
        out_specs.append(pl.BlockSpec((tm, LANES), lambda i, f: (i, 0)))
        out_shape.append(jax.ShapeDtypeStruct((m, LANES), F32))
    return pl.pallas_call(
        functools.partial(_ffn_kernel, final_norm=final_norm),
        grid=(m // tm, dff // tf),
        in_specs=[
            pl.BlockSpec(memory_space=pl.ANY),
            pl.BlockSpec((1, d), lambda i, f: (0, 0)),
            pl.BlockSpec((None, d, tf), lambda i, f: (idx, 0, f)),
            pl.BlockSpec((None, d, tf), lambda i, f: (idx, 0, f)),
            pl.BlockSpec((None, tf, d), lambda i, f: (idx, f, 0)),
            pl.BlockSpec((1, d), lambda i, f: (0, 0)),
        ],
        out_specs=out_specs,
        out_shape=out_shape,
        scratch_shapes=[pltpu.VMEM((tm, d), BF16), pltpu.VMEM((tm, d), F32), pltpu.SemaphoreType.DMA(())],
        compiler_params=_params("arbitrary", "arbitrary"),
        name="ffn",
    )(x, g.reshape(1, d), w_gate, w_up, w_down, final_g.reshape(1, d))


def _fnet_mix_kernel(t_ref, cs_ref, x_ref, r_ref, g_ref, o_ref, p_ref, uv_ref, y_ref):
    q = t_ref.shape[1]
    tn = o_ref.shape[-1]
    gd = cs_ref.shape[0]

    def fold(rows):
        hq = []
        for m in range(FFT_RADIX):
            rs = pl.ds(m * q + rows.start, rows.size)
            rstd = jnp.concatenate([r_ref[rs, :]] * (tn // LANES), axis=1)
            hq.append(x_ref[rs, :] * rstd * g_ref[...])
        e, o, d, qq = hq[0] + hq[2], hq[1] + hq[3], hq[0] - hq[2], hq[1] - hq[3]
        for idx, val in enumerate((e + o, e - o, d, qq)):
            p_ref[idx, rows, :] = val.astype(BF16)

    _for_row_chunks(q, fold)

    for grp in range(tn // gd):
        cols = slice(grp * gd, (grp + 1) * gd)
        res = [jnp.dot(p_ref[idx, :, cols], cs_ref[...], preferred_element_type=F32) for idx in range(FFT_RADIX)]
        c = [r[:, :gd] for r in res]
        s = [r[:, gd:] for r in res]
        u = (c[0], c[2] - s[3], c[1], c[2] + s[3])
        v = (s[0], s[2] + c[3], s[1], s[2] - c[3])
        for r in range(FFT_RADIX):
            uv_ref[r, 0:q, cols] = u[r].astype(BF16)
            uv_ref[r, q:2 * q, cols] = v[r].astype(BF16)

    for r in range(FFT_RADIX):
        y_ref[r] = jnp.dot(t_ref[r], uv_ref[r], preferred_element_type=F32).astype(BF16)
    rows = q // FFT_RADIX
    for kt in range(FFT_RADIX):
        blk = y_ref[:, kt * rows:(kt + 1) * rows, :]
        o_ref[kt * q:(kt + 1) * q, :] = jnp.swapaxes(blk, 0, 1).reshape(q, blk.shape[-1])


def _fnet_mix(tables, cs, x, rstd, g, *, tn):
    bsz, seq, d = x.shape
    q = seq // FFT_RADIX
    gd = cs.shape[0]
    return pl.pallas_call(
        _fnet_mix_kernel,
        grid=(bsz, d // tn),
        in_specs=[
            pl.BlockSpec((FFT_RADIX, q, 2 * q), lambda bb, j: (0, 0, 0), pipeline_mode=pl.Buffered(1)),
            pl.BlockSpec((gd, 2 * gd), lambda bb, j: (0, 0)),
            pl.BlockSpec((None, seq, tn), lambda bb, j: (bb, 0, j)),
            pl.BlockSpec((None, seq, LANES), lambda bb, j: (bb, 0, 0)),
            pl.BlockSpec((1, tn), lambda bb, j: (0, j)),
        ],
        out_specs=pl.BlockSpec((None, seq, tn), lambda bb, j: (bb, 0, j)),
        out_shape=jax.ShapeDtypeStruct((bsz, seq, d), BF16),
        scratch_shapes=[pltpu.VMEM((FFT_RADIX, q, tn), BF16), pltpu.VMEM((FFT_RADIX, 2 * q, tn), BF16),
                        pltpu.VMEM((FFT_RADIX, q, tn), BF16)],
        compiler_params=_params("parallel", "parallel"),
        name="fnet_mix",
    )(tables, cs, x, rstd, g.reshape(1, d))


def _cos_sin(rows, cols, n):
    ang = ((rows[:, None] * cols[None, :]) % n).astype(F32) * (2.0 * math.pi / n)
    return jnp.cos(ang), jnp.sin(ang)


def _dft_tables(n, scale):
    q = 1 << (int(math.log2(n)) // 2)
    cols = jnp.arange(n, dtype=jnp.int32)
    ca, sa = _cos_sin(jnp.arange(n // q, dtype=jnp.int32) * q, cols, n)
    cb, sb = _cos_sin(jnp.arange(q, dtype=jnp.int32), cols, n)
    ca, sa = (ca * scale)[:, None, :], (sa * scale)[:, None, :]
    cb, sb = cb[None, :, :], sb[None, :, :]
    return (ca * cb - sa * sb).reshape(n, n), (sa * cb + ca * sb).reshape(n, n)


def _radix_tables(n):
    q = n // FFT_RADIX
    c4, s4 = _dft_tables(q, 1.0 / math.sqrt(n))
    cr, sr = _cos_sin(jnp.arange(FFT_RADIX, dtype=jnp.int32), jnp.arange(q, dtype=jnp.int32), n)
    cr, sr = cr[:, None, :], sr[:, None, :]
    c = c4[None] * cr - s4[None] * sr
    sn = s4[None] * cr + c4[None] * sr
    return jnp.concatenate([c, -sn], axis=2)


def kernel(x, mix_norm_g, ffn_norm_g, final_norm_g, ab_w_in, conv_dw_w, conv_dw_b, conv_ln_g, conv_ln_b,
           sgu_ln_g, sgu_ln_b, sgu_w, sgu_b, ab_w_out, fnet_w_out, fnet_b_out, ffn_w_gate, ffn_w_up, ffn_w_down):
    bsz, seq, d = x.shape
    m = bsz * seq
    depth = mix_norm_g.shape[0]
    xf = x.reshape(m, d)
    for layer in range(depth):
        if layer % 2 == 0:
            i = layer // 2
            y_sgu, h = _sgu_branch(xf, mix_norm_g[layer], ab_w_in, i, sgu_ln_g[i], sgu_ln_b[i], sgu_w[i], sgu_b[i],
                                   tm=SGU_ROWS)
            cast = [(ab_w_out, i)] + ([(fnet_w_out, i)] if layer + 1 < depth else [])
            y_conv, w_bf16 = _conv_branch(h.reshape(bsz, seq, d), ab_w_in, i, conv_dw_w[i], conv_dw_b[i], conv_ln_g[i],
                                          conv_ln_b[i], cast)
            xf = _proj_residual([y_conv.reshape(m, -1), y_sgu], w_bf16[0], xf, tm=PROJ_ROWS)
        else:
            j = layer // 2
            gd = d // FNET_GROUPS
            cc, sc = _dft_tables(gd, 1.0 / math.sqrt(gd))
            y = _fnet_mix(_radix_tables(seq).astype(BF16), jnp.concatenate([cc, sc], axis=1).astype(BF16),
                          xf.reshape(bsz, seq, d), rstd.reshape(bsz, seq, LANES), mix_norm_g[layer], tn=FNET_COLS)
            xf = _proj_residual([y.reshape(m, d)], w_bf16[1], xf, fnet_b_out[j], tm=PROJ_ROWS)
        outs = _ffn(xf, ffn_norm_g[layer], ffn_w_gate, ffn_w_up, ffn_w_down, layer, final_norm_g,
                    tm=FFN_ROWS, tf=FFN_HIDDEN, final_norm=layer == depth - 1)
        xf = outs[0]
        rstd = outs[1] if len(outs) > 1 else None
    return xf.reshape(bsz, seq, d)
```

```python
import functools
import math

import jax
import jax.numpy as jnp
from jax import lax
from jax.experimental import pallas as pl
from jax.experimental.pallas import tpu as pltpu

F32 = jnp.float32
BF16 = jnp.bfloat16

RMS_EPS = 1e-6
LN_EPS = 1e-5
CONV_GROUP_DIM = 128
CONV_WIDTH = 31
CONV_PAD = (CONV_WIDTH - 1) // 2
CONV_HALO = 16
CONV_ROWS = 64
CONV_BLOCK = 256
SUBLANES = 8
LANES = 128
SGU_HEADS = 8
CHUNK = 128
FNET_GROUPS = 8
FFT_RADIX = 4

V7X_VMEM_BYTES = 64 * 1024 * 1024
VMEM_LIMIT_BYTES = V7X_VMEM_BYTES - 3 * 1024 * 1024
ROW_CHUNK = 256

SGU_ROWS = 512
PROJ_ROWS = 512
FFN_ROWS = 1024
FFN_HIDDEN = 512
FNET_COLS = 1024


def _params(*semantics):
    return pltpu.CompilerParams(dimension_semantics=semantics, vmem_limit_bytes=VMEM_LIMIT_BYTES)


def _rms(x, g):
    return x * lax.rsqrt(jnp.mean(x * x, axis=-1, keepdims=True) + RMS_EPS) * g


def _layer_norm(x, g, b):
    mu = jnp.mean(x, axis=-1, keepdims=True)
    xc = x - mu
    return xc * lax.rsqrt(jnp.mean(xc * xc, axis=-1, keepdims=True) + LN_EPS) * g + b


def _gelu(x):
    return 0.5 * x * (1.0 + lax.erf(x * (1.0 / math.sqrt(2.0))))


def _silu(x):
    return x * jax.nn.sigmoid(x)


def _for_row_chunks(nrows, fn):
    def body(r, carry):
        fn(pl.ds(pl.multiple_of(r * ROW_CHUNK, ROW_CHUNK), ROW_CHUNK))
        return carry
    lax.fori_loop(0, nrows // ROW_CHUNK, body, 0)


def _cast_weight(w_ref, wb_ref):
    def cast(rows):
        wb_ref[rows, :] = w_ref[rows, :].astype(BF16)
    _for_row_chunks(w_ref.shape[0], cast)


def _sgu_branch_kernel(*refs, tm, nt, n_cast):
    x_ref, g_ref, w_ref, lg_ref, lb_ref, ws_ref, bs_ref = refs[:7]
    cast_in = refs[7:7 + n_cast]
    o_ref, h_ref = refs[7 + n_cast:9 + n_cast]
    cast_out = refs[9 + n_cast:9 + 2 * n_cast]
    wb_ref, z_a, z_b = refs[9 + 2 * n_cast:]
    t = pl.program_id(0)
    gdim = o_ref.shape[-1]
    hd = gdim // SGU_HEADS

    for src, dst in zip(cast_in, cast_out):
        dst[...] = src[...].astype(BF16)

    def project(z_w):
        for r in range(tm // ROW_CHUNK):
            rows = slice(r * ROW_CHUNK, (r + 1) * ROW_CHUNK)
            h = _rms(x_ref[rows, :], g_ref[...]).astype(BF16)
            h_ref[rows, :] = h
            z_w[rows, :] = jnp.dot(h, wb_ref[...], preferred_element_type=F32)

    def gate(z_r):
        for c in range(tm // CHUNK):
            rows = slice(c * CHUNK, (c + 1) * CHUNK)
            v = _layer_norm(_gelu(z_r[rows, gdim:]), lg_ref[...], lb_ref[...]).astype(BF16)
            for hh in range(SGU_HEADS):
                cols = slice(hh * hd, (hh + 1) * hd)
                mixed = jnp.dot(ws_ref[hh].astype(BF16), v[:, cols], preferred_element_type=F32)
                mixed = mixed + bs_ref[:, hh:hh + 1]
                o_ref[rows, cols] = (_gelu(z_r[rows, cols]) * mixed).astype(o_ref.dtype)

    @pl.when(t == 0)
    def _():
        _cast_weight(w_ref, wb_ref)
        project(z_a)

    for parity, (z_w, z_r) in enumerate(((z_a, z_b), (z_b, z_a))):
        @pl.when((t > 0) & (t < nt) & (t % 2 == parity))
        def _():
            project(z_w)
            gate(z_r)

    @pl.when(t == nt)
    def _():
        gate(z_a if nt % 2 == 1 else z_b)


def _sgu_branch(x, g, w_in, idx, ln_g, ln_b, w_s, b_s, cast_weights, *, tm):
    m, d = x.shape
    gdim = ln_g.shape[0]
    wblk = w_in.shape[2] // (2 * gdim) - 1
    nt = m // tm

    def proj(t):
        return jnp.minimum(t, nt - 1)

    def gated(t):
        return jnp.maximum(t - 1, 0)

    cast_in_specs, cast_out_specs, cast_out_shapes = [], [], []
    for w, li in cast_weights:
        _, k, n = w.shape
        cast_in_specs.append(pl.BlockSpec((None, k // nt, n), lambda t, li=li: (li, proj(t), 0)))
        cast_out_specs.append(pl.BlockSpec((k // nt, n), lambda t: (proj(t), 0)))
        cast_out_shapes.append(jax.ShapeDtypeStruct((k, n), BF16))
    outs = pl.pallas_call(
        functools.partial(_sgu_branch_kernel, tm=tm, nt=nt, n_cast=len(cast_weights)),
        grid=(nt + 1,),
        in_specs=[
            pl.BlockSpec((tm, d), lambda t: (proj(t), 0)),
            pl.BlockSpec((1, d), lambda t: (0, 0)),
            pl.BlockSpec((None, d, 2 * gdim), lambda t: (idx, 0, wblk), pipeline_mode=pl.Buffered(1)),
            pl.BlockSpec((1, gdim), lambda t: (0, 0)),
            pl.BlockSpec((1, gdim), lambda t: (0, 0)),
            pl.BlockSpec((SGU_HEADS, CHUNK, CHUNK), lambda t: (0, 0, 0)),
            pl.BlockSpec((CHUNK, SGU_HEADS), lambda t: (0, 0)),
        ] + cast_in_specs,
        out_specs=[pl.BlockSpec((tm, gdim), lambda t: (gated(t), 0)),
                   pl.BlockSpec((tm, d), lambda t: (proj(t), 0))] + cast_out_specs,
        out_shape=[jax.ShapeDtypeStruct((m, gdim), BF16), jax.ShapeDtypeStruct((m, d), BF16)] + cast_out_shapes,
        scratch_shapes=[pltpu.VMEM((d, 2 * gdim), BF16), pltpu.VMEM((tm, 2 * gdim), F32),
                        pltpu.VMEM((tm, 2 * gdim), F32)],
        compiler_params=_params("arbitrary"),
        name="sgu_branch",
    )(x, g.reshape(1, d), w_in, ln_g.reshape(1, gdim), ln_b.reshape(1, gdim), w_s, b_s.T,
      *[w for w, _ in cast_weights])
    return outs[0], outs[1], outs[2:]


def _conv_step(h_ref, wa_ref, wg_ref, dwb_ref, b_ref, lg_ref, lb_ref, o_ref, stage_w, stage_r, *, seq):
    cw = o_ref.shape[-1]
    gd = CONV_GROUP_DIM
    w = jnp.concatenate([wa_ref[...].astype(BF16), wg_ref[...].astype(BF16)], axis=1)

    def glu(c):
        z = jnp.dot(h_ref[c * ROW_CHUNK:(c + 1) * ROW_CHUNK, :], w, preferred_element_type=F32)
        start = CONV_HALO + c * ROW_CHUNK
        u = z[:, :cw] * jax.nn.sigmoid(z[:, cw:])
        for grp in range(cw // gd):
            stage_w[grp, start:start + ROW_CHUNK, :] = u[:, grp * gd:(grp + 1) * gd]

    def conv(c):
        for s in range(ROW_CHUNK // CONV_ROWS):
            r0 = c * ROW_CHUNK + s * CONV_ROWS
            base = r0 + CONV_HALO - CONV_PAD
            for grp in range(cw // gd):
                cols = slice(grp * gd, (grp + 1) * gd)
                acc = jnp.zeros((CONV_ROWS // SUBLANES, SUBLANES, gd), F32)
                for k in range(CONV_WIDTH):
                    tap = stage_r[grp, base + k:base + k + CONV_ROWS, :].reshape(CONV_ROWS // SUBLANES, SUBLANES, gd)
                    acc = acc + tap * dwb_ref[k * SUBLANES:(k + 1) * SUBLANES, cols]
                y = _layer_norm(acc.reshape(CONV_ROWS, gd) + b_ref[:, cols], lg_ref[:, cols], lb_ref[:, cols])
                o_ref[r0:r0 + CONV_ROWS, cols] = _silu(y).astype(o_ref.dtype)

    for c in range(seq // ROW_CHUNK):
        glu(c)
        conv(c)


def _conv_branch_kernel(*refs, seq, n_cast):
    h_ref, wa_ref, wg_ref, dw_ref, b_ref, lg_ref, lb_ref = refs[:7]
    cast_in = refs[7:7 + n_cast]
    o_ref = refs[7 + n_cast]
    cast_out = refs[8 + n_cast:8 + 2 * n_cast]
    stage_a, stage_b, dwb_ref = refs[8 + 2 * n_cast:]
    t = pl.program_id(0)
    cw = o_ref.shape[-1]

    @pl.when(t == 0)
    def _():
        stage_a[...] = jnp.zeros_like(stage_a)
        stage_b[...] = jnp.zeros_like(stage_b)

    for src, dst in zip(cast_in, cast_out):
        dst[...] = src[...].astype(BF16)

    for k in range(CONV_WIDTH):
        dwb_ref[k * SUBLANES:(k + 1) * SUBLANES, :] = jnp.broadcast_to(dw_ref[k:k + 1, :], (SUBLANES, cw))

    step = functools.partial(_conv_step, h_ref, wa_ref, wg_ref, dwb_ref, b_ref, lg_ref, lb_ref, o_ref, seq=seq)

    @pl.when(t % 2 == 0)
    def _():
        step(stage_a, stage_b)

    @pl.when(t % 2 == 1)
    def _():
        step(stage_b, stage_a)


def _conv_branch(h, w_in, idx, dw_w, dw_b, ln_g, ln_b, cast_weights):
    bsz, seq, d = h.shape
    cdim = dw_w.shape[1]
    cw = CONV_BLOCK
    nblk = cdim // cw
    items = bsz * nblk

    def proj(t):
        return jnp.minimum(t, items - 1)

    def conv(t):
        return jnp.maximum(t - 1, 0)

    vec = pl.BlockSpec((1, cw), lambda t: (0, conv(t) % nblk))
    stage = (cw // CONV_GROUP_DIM, seq + 2 * CONV_HALO, CONV_GROUP_DIM)
    cast_in_specs, cast_out_specs, cast_out_shapes = [], [], []
    for w, li in cast_weights:
        _, k, n = w.shape
        cast_in_specs.append(pl.BlockSpec((None, k // items, n), lambda t, li=li: (li, proj(t), 0)))
        cast_out_specs.append(pl.BlockSpec((k // items, n), lambda t: (proj(t), 0)))
        cast_out_shapes.append(jax.ShapeDtypeStruct((k, n), BF16))
    outs = pl.pallas_call(
        functools.partial(_conv_branch_kernel, seq=seq, n_cast=len(cast_weights)),
        grid=(items + 1,),
        in_specs=[
            pl.BlockSpec((None, seq, d), lambda t: (proj(t) // nblk, 0, 0)),
            pl.BlockSpec((None, d, cw), lambda t: (idx, 0, proj(t) % nblk)),
            pl.BlockSpec((None, d, cw), lambda t: (idx, 0, nblk + proj(t) % nblk)),
            pl.BlockSpec((CONV_WIDTH, cw), lambda t: (0, conv(t) % nblk)),
            vec, vec, vec,
        ] + cast_in_specs,
        out_specs=[pl.BlockSpec((None, seq, cw), lambda t: (conv(t) // nblk, 0, conv(t) % nblk))] + cast_out_specs,
        out_shape=[jax.ShapeDtypeStruct((bsz, seq, cdim), BF16)] + cast_out_shapes,
        scratch_shapes=[pltpu.VMEM(stage, F32), pltpu.VMEM(stage, F32), pltpu.VMEM((CONV_WIDTH * SUBLANES, cw), F32)],
        compiler_params=_params("arbitrary"),
        name="conv_branch",
    )(h, w_in, w_in, dw_w, dw_b.reshape(1, cdim), ln_g.reshape(1, cdim), ln_b.reshape(1, cdim),
      *[w for w, _ in cast_weights])
    return outs[0], outs[1:]


def _proj_residual_kernel(*refs, n_lhs, has_bias):
    lhs_refs = refs[:n_lhs]
    w_ref, x_ref = refs[n_lhs], refs[n_lhs + 1]
    o_ref = refs[-1]
    acc = x_ref[...]
    if has_bias:
        acc = acc + refs[n_lhs + 2][...]
    k0 = 0
    for lhs_ref in lhs_refs:
        kw = lhs_ref.shape[-1]
        acc = acc + jnp.dot(lhs_ref[...], w_ref[k0:k0 + kw, :], preferred_element_type=F32)
        k0 += kw
    o_ref[...] = acc


def _proj_residual(lhs_list, w, x, bias=None, *, tm):
    m, n = x.shape
    k = w.shape[0]
    in_specs = [pl.BlockSpec((tm, lhs.shape[1]), lambda i: (i, 0)) for lhs in lhs_list]
    in_specs += [pl.BlockSpec((k, n), lambda i: (0, 0), pipeline_mode=pl.Buffered(1)),
                 pl.BlockSpec((tm, n), lambda i: (i, 0))]
    args = list(lhs_list) + [w, x]
    if bias is not None:
        in_specs.append(pl.BlockSpec((1, n), lambda i: (0, 0)))
        args.append(bias.reshape(1, n))
    return pl.pallas_call(
        functools.partial(_proj_residual_kernel, n_lhs=len(lhs_list), has_bias=bias is not None),
        grid=(m // tm,),
        in_specs=in_specs,
        out_specs=pl.BlockSpec((tm, n), lambda i: (i, 0)),
        out_shape=jax.ShapeDtypeStruct((m, n), F32),
        compiler_params=_params("parallel"),
        name="proj_residual",
    )(*args)


def _ffn_kernel(x_hbm, g_ref, wg_ref, wu_ref, wd_ref, fg_ref, o_ref, *rest, final_norm):
    if final_norm:
        h_ref, x_buf, x_sem = rest
    else:
        rstd_ref, h_ref, x_buf, x_sem = rest
    i = pl.program_id(0)
    f = pl.program_id(1)
    tm = x_buf.shape[0]

    def x_copy(tile):
        return pltpu.make_async_copy(x_hbm.at[pl.ds(pl.multiple_of(tile * tm, tm), tm), :], x_buf, x_sem)

    @pl.when((i == 0) & (f == 0))
    def _():
        x_copy(0).start()

    last = pl.num_programs(1) - 1

    def delta():
        tf = wg_ref.shape[-1]
        w_gu = jnp.concatenate([wg_ref[...].astype(BF16), wu_ref[...].astype(BF16)], axis=1)
        gu = jnp.dot(h_ref[...], w_gu, preferred_element_type=F32)
        act = (_silu(gu[:, :tf]) * gu[:, tf:]).astype(BF16)
        return jnp.dot(act, wd_ref[...].astype(BF16), preferred_element_type=F32)

    @pl.when(f == 0)
    def _():
        x_copy(i).wait()
        h_ref[...] = _rms(x_buf[...], g_ref[...]).astype(BF16)
        o_ref[...] = x_buf[...] + delta()

    @pl.when((f == 1) & (i + 1 < pl.num_programs(0)))
    def _():
        x_copy(i + 1).start()

    if final_norm:
        @pl.when(f > 0)
        def _():
            o_ref[...] += delta()

        @pl.when(f == last)
        def _():
            o_ref[...] = _rms(o_ref[...], fg_ref[...])
    else:
        @pl.when((f > 0) & (f < last))
        def _():
            o_ref[...] += delta()

        @pl.when(f == last)
        def _():
            y = o_ref[...] + delta()
            o_ref[...] = y
            rstd = lax.rsqrt(jnp.mean(y * y, axis=-1, keepdims=True) + RMS_EPS)
            rstd_ref[...] = jnp.broadcast_to(rstd, rstd_ref.shape)


def _ffn(x, g, w_gate, w_up, w_down, idx, final_g, *, tm, tf, final_norm):
    m, d = x.shape
    dff = w_gate.shape[2]
    assert dff // tf >= 2, "the first and the last hidden step must be different steps"
    out_specs = [pl.BlockSpec((tm, d), lambda i, f: (i, 0))]
    out_shape = [jax.ShapeDtypeStruct((m, d), F32)]
    if not final_norm:
        out_specs.append(pl.BlockSpec((tm, LANES), lambda i, f: (i, 0)))
        out_shape.append(jax.ShapeDtypeStruct((m, LANES), F32))
    return pl.pallas_call(
        functools.partial(_ffn_kernel, final_norm=final_norm),
        grid=(m // tm, dff // tf),
        in_specs=[
            pl.BlockSpec(memory_space=pl.ANY),
            pl.BlockSpec((1, d), lambda i, f: (0, 0)),
            pl.BlockSpec((None, d, tf), lambda i, f: (idx, 0, f)),
            pl.BlockSpec((None, d, tf), lambda i, f: (idx, 0, f)),
            pl.BlockSpec((None, tf, d), lambda i, f: (idx, f, 0)),
            pl.BlockSpec((1, d), lambda i, f: (0, 0)),
        ],
        out_specs=out_specs,
        out_shape=out_shape,
        scratch_shapes=[pltpu.VMEM((tm, d), BF16), pltpu.VMEM((tm, d), F32), pltpu.SemaphoreType.DMA(())],
        compiler_params=_params("arbitrary", "arbitrary"),
        name="ffn",
    )(x, g.reshape(1, d), w_gate, w_up, w_down, final_g.reshape(1, d))


def _fnet_mix_kernel(t_ref, cs_ref, x_ref, r_ref, g_ref, o_ref, p_ref, uv_ref, y_ref):
    q = t_ref.shape[1]
    tn = o_ref.shape[-1]
    gd = cs_ref.shape[0]

    def fold(rows):
        hq = []
        for m in range(FFT_RADIX):
            rs = pl.ds(m * q + rows.start, rows.size)
            rstd = jnp.concatenate([r_ref[rs, :]] * (tn // LANES), axis=1)
            hq.append(x_ref[rs, :] * rstd * g_ref[...])
        e, o, d, qq = hq[0] + hq[2], hq[1] + hq[3], hq[0] - hq[2], hq[1] - hq[3]
        for idx, val in enumerate((e + o, e - o, d, qq)):
            p_ref[idx, rows, :] = val.astype(BF16)

    _for_row_chunks(q, fold)

    for grp in range(tn // gd):
        cols = slice(grp * gd, (grp + 1) * gd)
        res = [jnp.dot(p_ref[idx, :, cols], cs_ref[...], preferred_element_type=F32) for idx in range(FFT_RADIX)]
        c = [r[:, :gd] for r in res]
        s = [r[:, gd:] for r in res]
        u = (c[0], c[2] - s[3], c[1], c[2] + s[3])
        v = (s[0], s[2] + c[3], s[1], s[2] - c[3])
        for r in range(FFT_RADIX):
            uv_ref[r, 0:q, cols] = u[r].astype(BF16)
            uv_ref[r, q:2 * q, cols] = v[r].astype(BF16)

    for r in range(FFT_RADIX):
        y_ref[r] = jnp.dot(t_ref[r], uv_ref[r], preferred_element_type=F32).astype(BF16)
    rows = q // FFT_RADIX
    for kt in range(FFT_RADIX):
        blk = y_ref[:, kt * rows:(kt + 1) * rows, :]
        o_ref[kt * q:(kt + 1) * q, :] = jnp.swapaxes(blk, 0, 1).reshape(q, blk.shape[-1])


def _fnet_mix(tables, cs, x, rstd, g, *, tn):
    bsz, seq, d = x.shape
    q = seq // FFT_RADIX
    gd = cs.shape[0]
    return pl.pallas_call(
        _fnet_mix_kernel,
        grid=(bsz, d // tn),
        in_specs=[
            pl.BlockSpec((FFT_RADIX, q, 2 * q), lambda bb, j: (0, 0, 0), pipeline_mode=pl.Buffered(1)),
            pl.BlockSpec((gd, 2 * gd), lambda bb, j: (0, 0)),
            pl.BlockSpec((None, seq, tn), lambda bb, j: (bb, 0, j)),
            pl.BlockSpec((None, seq, LANES), lambda bb, j: (bb, 0, 0)),
            pl.BlockSpec((1, tn), lambda bb, j: (0, j)),
        ],
        out_specs=pl.BlockSpec((None, seq, tn), lambda bb, j: (bb, 0, j)),
        out_shape=jax.ShapeDtypeStruct((bsz, seq, d), BF16),
        scratch_shapes=[pltpu.VMEM((FFT_RADIX, q, tn), BF16), pltpu.VMEM((FFT_RADIX, 2 * q, tn), BF16),
                        pltpu.VMEM((FFT_RADIX, q, tn), BF16)],
        compiler_params=_params("parallel", "parallel"),
        name="fnet_mix",
    )(tables, cs, x, rstd, g.reshape(1, d))


def _cos_sin(rows, cols, n):
    ang = ((rows[:, None] * cols[None, :]) % n).astype(F32) * (2.0 * math.pi / n)
    return jnp.cos(ang), jnp.sin(ang)


def _dft_tables(n, scale):
    q = 1 << (int(math.log2(n)) // 2)
    cols = jnp.arange(n, dtype=jnp.int32)
    ca, sa = _cos_sin(jnp.arange(n // q, dtype=jnp.int32) * q, cols, n)
    cb, sb = _cos_sin(jnp.arange(q, dtype=jnp.int32), cols, n)
    ca, sa = (ca * scale)[:, None, :], (sa * scale)[:, None, :]
    cb, sb = cb[None, :, :], sb[None, :, :]
    return (ca * cb - sa * sb).reshape(n, n), (sa * cb + ca * sb).reshape(n, n)


def _radix_tables(n):
    q = n // FFT_RADIX
    c4, s4 = _dft_tables(q, 1.0 / math.sqrt(n))
    cr, sr = _cos_sin(jnp.arange(FFT_RADIX, dtype=jnp.int32), jnp.arange(q, dtype=jnp.int32), n)
    cr, sr = cr[:, None, :], sr[:, None, :]
    c = c4[None] * cr - s4[None] * sr
    sn = s4[None] * cr + c4[None] * sr
    return jnp.concatenate([c, -sn], axis=2)


def kernel(x, mix_norm_g, ffn_norm_g, final_norm_g, ab_w_in, conv_dw_w, conv_dw_b, conv_ln_g, conv_ln_b,
           sgu_ln_g, sgu_ln_b, sgu_w, sgu_b, ab_w_out, fnet_w_out, fnet_b_out, ffn_w_gate, ffn_w_up, ffn_w_down):
    bsz, seq, d = x.shape
    m = bsz * seq
    depth = mix_norm_g.shape[0]
    xf = x.reshape(m, d)
    for layer in range(depth):
        if layer % 2 == 0:
            i = layer // 2
            cast = [(ab_w_out, i)] + ([(fnet_w_out, i)] if layer + 1 < depth else [])
            y_sgu, h, w_bf16 = _sgu_branch(xf, mix_norm_g[layer], ab_w_in, i, sgu_ln_g[i], sgu_ln_b[i], sgu_w[i],
                                           sgu_b[i], cast, tm=SGU_ROWS)
            y_conv, _ = _conv_branch(h.reshape(bsz, seq, d), ab_w_in, i, conv_dw_w[i], conv_dw_b[i], conv_ln_g[i],
                                     conv_ln_b[i], [])
            xf = _proj_residual([y_conv.reshape(m, -1), y_sgu], w_bf16[0], xf, tm=PROJ_ROWS)
        else:
            j = layer // 2
            gd = d // FNET_GROUPS
            cc, sc = _dft_tables(gd, 1.0 / math.sqrt(gd))
            y = _fnet_mix(_radix_tables(seq).astype(BF16), jnp.concatenate([cc, sc], axis=1).astype(BF16),
                          xf.reshape(bsz, seq, d), rstd.reshape(bsz, seq, LANES), mix_norm_g[layer], tn=FNET_COLS)
            xf = _proj_residual([y.reshape(m, d)], w_bf16[1], xf, fnet_b_out[j], tm=PROJ_ROWS)
        outs = _ffn(xf, ffn_norm_g[layer], ffn_w_gate, ffn_w_up, ffn_w_down, layer, final_norm_g,
                    tm=FFN_ROWS, tf=FFN_HIDDEN, final_norm=layer == depth - 1)
        xf = outs[0]
        rstd = outs[1] if len(outs) > 1 else None
    return xf.reshape(bsz, seq, d)
```
